```python
import jax
import jax.numpy as jnp
from jax import lax
import numpy as np


D_MODEL = 2048
BATCH = 4
SEQ = 2048
DEPTH = 4

CTX_LEN = 256
GRID_W = 64
RMS_EPS = 1e-6
N_BRANCH = 3
BRANCH_DIM = 1024

POOL_WINDOWS = (2, 4, 8, 16)
N_POOL_GROUPS = len(POOL_WINDOWS)
POOL_DIM = BRANCH_DIM
POOL_GROUP = POOL_DIM // N_POOL_GROUPS

MLA_HEADS = 8
MLA_Q_LORA = 512
MLA_KV_LORA = 512
MLA_NOPE = 128
MLA_ROPE = 64
MLA_V = BRANCH_DIM // MLA_HEADS
MLA_QK = MLA_NOPE + MLA_ROPE
ROPE_BASE = 10000.0
Q_BLOCK = 128

GLA_HEADS = 4
GLA_DK = 256
GLA_DV = BRANCH_DIM // GLA_HEADS
GLA_GATE_RANK = 16
GLA_GATE_TAU = 16.0
GLA_CHUNK = 64

FFN_HIDDEN = -(-8 * D_MODEL // (3 * 256)) * 256

KV_SPLITS = (MLA_KV_LORA, MLA_ROPE, GLA_HEADS * GLA_DK, GLA_HEADS * GLA_DV, GLA_GATE_RANK, GLA_GATE_RANK)
Q_SPLITS = (MLA_Q_LORA, GLA_HEADS * GLA_DK, GLA_HEADS * GLA_DV, POOL_DIM, N_BRANCH * D_MODEL)
KV_COLS = sum(KV_SPLITS)
IN_COLS = KV_COLS + sum(Q_SPLITS)

kernel_name = "hybrid_pool_mla_gla_dit_block"


def rms_norm(x, g):
    xf = x.astype(jnp.float32)
    y = xf * lax.rsqrt(jnp.mean(xf * xf, axis=-1, keepdims=True) + RMS_EPS)
    return (y * g.astype(jnp.float32)).astype(x.dtype)


def modulate(h, shift, scale):
    return h * (1.0 + scale) + shift


def split_cols(z, sizes):
    return jnp.split(z, [int(i) for i in np.cumsum(sizes)[:-1]], axis=-1)


def to_heads(t, n_heads):
    b, s, _ = t.shape
    return t.reshape(b, s, n_heads, -1).transpose(0, 2, 1, 3)


def from_heads(t):
    b, h, s, d = t.shape
    return t.transpose(0, 2, 1, 3).reshape(b, s, h * d)


def flip_seq(t):
    return jnp.flip(t, axis=2)


def axial_rope_tables(n_rows, dtype):
    half = MLA_ROPE // 2
    inv_freq = 1.0 / (ROPE_BASE ** (jnp.arange(0, half, 2, dtype=jnp.float32) / half))
    row = jnp.repeat(jnp.arange(n_rows), GRID_W).astype(jnp.float32)
    col = jnp.tile(jnp.arange(GRID_W), n_rows).astype(jnp.float32)
    ang_r = row[:, None] * inv_freq[None, :]
    ang_c = col[:, None] * inv_freq[None, :]
    return tuple(t.astype(dtype)[None, :, None, :] for t in
                 (jnp.cos(ang_r), jnp.sin(ang_r), jnp.cos(ang_c), jnp.sin(ang_c)))


def rotate(x, cos, sin):
    x1, x2 = jnp.split(x, 2, axis=-1)
    return jnp.concatenate([x1 * cos - x2 * sin, x2 * cos + x1 * sin], axis=-1)


def apply_axial_rope(t, rope):
    cos_r, sin_r, cos_c, sin_c = rope
    t_nope, t_row, t_col = split_cols(t, (MLA_NOPE, MLA_ROPE // 2, MLA_ROPE // 2))
    return jnp.concatenate([t_nope, rotate(t_row, cos_r, sin_r), rotate(t_col, cos_c, sin_c)], axis=-1)


def mla_queries(cq, q_norm_g, w_q_up, q_head_g, rope):
    b, s, _ = cq.shape
    q = (rms_norm(cq, q_norm_g) @ w_q_up).reshape(b, s, MLA_HEADS, MLA_QK)
    q = rms_norm(q, q_head_g)
    if rope is not None:
        q = apply_axial_rope(q, rope)
    return q.transpose(0, 2, 1, 3)


def mla_keys_values(ckv, krope, kv_norm_g, w_kv_up, k_head_g, rope):
    b, s, _ = ckv.shape
    kv = (rms_norm(ckv, kv_norm_g) @ w_kv_up).reshape(b, s, MLA_HEADS, MLA_NOPE + MLA_V)
    k_nope, v = kv[..., :MLA_NOPE], kv[..., MLA_NOPE:]
    k_rope = jnp.broadcast_to(krope[:, :, None, :], (b, s, MLA_HEADS, MLA_ROPE))
    k = rms_norm(jnp.concatenate([k_nope, k_rope], axis=-1), k_head_g)
    if rope is not None:
        k = apply_axial_rope(k, rope)
    return k.transpose(0, 2, 1, 3), v.transpose(0, 2, 1, 3)


def block_attention(q, k, v):
    b, h, s, dh = q.shape
    nb = s // Q_BLOCK
    scale = dh ** -0.5
    qb = q.reshape(b, h, nb, Q_BLOCK, dh).transpose(2, 0, 1, 3, 4)

    def attend(qi):
        sc = jnp.einsum('bhqd,bhkd->bhqk', qi, k).astype(jnp.float32) * scale
        p = jax.nn.softmax(sc, axis=-1).astype(v.dtype)
        return jnp.einsum('bhqk,bhkd->bhqd', p, v)

    o = lax.map(attend, qb)
    return o.transpose(1, 2, 0, 3, 4).reshape(b, h, s, v.shape[-1])


def gla_chunk_scan(q, k, v, log_a, s0):
    b, h, s, dk = q.shape
    dv = v.shape[-1]
    n = s // GLA_CHUNK

    def chunks(t):
        return t.reshape(b, h, n, GLA_CHUNK, t.shape[-1]).transpose(2, 0, 1, 3, 4)

    causal = jnp.tril(jnp.ones((GLA_CHUNK, GLA_CHUNK), dtype=bool))[:, :, None]

    def step(state, inp):
        qc, kc, vc, ac = inp
        cum = jnp.cumsum(ac, axis=2)
        inter = jnp.einsum('bhtd,bhde->bhte', qc * jnp.exp(cum), state)
        rel = jnp.where(causal, cum[:, :, :, None, :] - cum[:, :, None, :, :], -jnp.inf)
        scores = jnp.einsum('bhtd,bhsd,bhtsd->bhts', qc, kc, jnp.exp(rel))
        intra = jnp.einsum('bhts,bhse->bhte', scores, vc)
        last = cum[:, :, -1:, :]
        new_state = state * jnp.exp(last).swapaxes(-1, -2) + jnp.einsum(
            'bhsd,bhse->bhde', kc * jnp.exp(last - cum), vc)
        return new_state.astype(state.dtype), inter + intra

    s_fin, o = lax.scan(step, s0, (chunks(q), chunks(k), chunks(v), chunks(log_a)))
    return s_fin, o.transpose(1, 2, 0, 3, 4).reshape(b, h, s, dv)


def gla_final_state(k, v, log_a):
    cum = jnp.cumsum(log_a, axis=2)
    return jnp.einsum('bhsd,bhse->bhde', k * jnp.exp(cum[:, :, -1:, :] - cum), v)


def gla_output(o, gg, out_norm_g):
    b, h, s, dv = o.shape
    on = rms_norm(o.transpose(0, 2, 1, 3), out_norm_g)
    return on.reshape(b, s, h * dv) * jax.nn.silu(gg)


def multiscale_pool(u):
    b, s, _ = u.shape
    uf = u.astype(jnp.float32).reshape(b, s, N_POOL_GROUPS, POOL_GROUP)
    cs = jnp.concatenate([jnp.zeros_like(uf[:, :1]), jnp.cumsum(uf, axis=1)], axis=1)
    t = jnp.arange(s)
    outs = []
    for gi, w in enumerate(POOL_WINDOWS):
        lo = jnp.clip(t - w // 2, 0, s - 1)
        hi = jnp.clip(t + w // 2 - 1, 0, s - 1)
        win_sum = cs[:, hi + 1, gi] - cs[:, lo, gi]
        cnt = (hi - lo + 1).astype(jnp.float32)[None, :, None]
        outs.append(win_sum / cnt - uf[:, :, gi])
    return jnp.stack(outs, axis=2).astype(u.dtype)


def pool_branch(u, pool_w, pool_scale):
    y = jnp.einsum('btgc,gcd->btgd', multiscale_pool(u), pool_w)
    return y.reshape(u.shape) * pool_scale


def merge_branches(pool_o, mla_o, gla_o, gate_logits, w_branch, w_out):
    b, s, _ = pool_o.shape
    br = jnp.stack([pool_o, mla_o, gla_o], axis=2)
    proj = jnp.einsum('btnc,ncd->btnd', br, w_branch)
    gates = jax.nn.sigmoid(gate_logits.reshape(b, s, N_BRANCH, D_MODEL))
    return jnp.sum(gates * proj, axis=2) @ w_out


def swiglu(h, w1, w3, w2):
    return (jax.nn.silu(h @ w1) * (h @ w3)) @ w2


def hybrid_mixer(h_lat, h_ctx, need_ctx_out, rope, w_in, mla_q_norm_g, mla_w_q_up, mla_kv_norm_g,
                 mla_w_kv_up, mla_q_head_g, mla_k_head_g, gla_w_gate_up, gla_b_gate, gla_out_norm_g,
                 pool_w, pool_scale, w_branch, w_out):
    z_lat = h_lat @ w_in
    z_ctx = h_ctx @ (w_in if need_ctx_out else w_in[:, :KV_COLS])
    ckv_l, kr_l, gk_l, gv_l, lrf_l, lrb_l = split_cols(z_lat[..., :KV_COLS], KV_SPLITS)
    cq_l, gq_l, gg_l, pin_l, gate_l = split_cols(z_lat[..., KV_COLS:], Q_SPLITS)
    ckv_c, kr_c, gk_c, gv_c, lrf_c, lrb_c = split_cols(z_ctx[..., :KV_COLS], KV_SPLITS)

    k_l, v_l = mla_keys_values(ckv_l, kr_l, mla_kv_norm_g, mla_w_kv_up, mla_k_head_g, rope)
    k_c, v_c = mla_keys_values(ckv_c, kr_c, mla_kv_norm_g, mla_w_kv_up, mla_k_head_g, None)
    q_l = mla_queries(cq_l, mla_q_norm_g, mla_w_q_up, mla_q_head_g, rope)
    mla_l = from_heads(block_attention(q_l, jnp.concatenate([k_c, k_l], axis=2),
                                       jnp.concatenate([v_c, v_l], axis=2)))

    def log_decay(lr, d):
        return to_heads(jax.nn.log_sigmoid(lr @ gla_w_gate_up[d] + gla_b_gate[d]) / GLA_GATE_TAU, GLA_HEADS)

    q_scale = GLA_DK ** -0.5
    kg_c, vg_c = to_heads(gk_c, GLA_HEADS), to_heads(gv_c, GLA_HEADS)
    laf_c, lab_c = log_decay(lrf_c, 0), log_decay(lrb_c, 1)
    if need_ctx_out:
        cq_c, gq_c, gg_c, pin_c, gate_c = split_cols(z_ctx[..., KV_COLS:], Q_SPLITS)
        qg_c = to_heads(gq_c, GLA_HEADS) * q_scale
        zero = jnp.zeros(kg_c.shape[:2] + (GLA_DK, GLA_DV), vg_c.dtype)
        s_cf, o_cf = gla_chunk_scan(qg_c, kg_c, vg_c, laf_c, zero)
        s_cb, o_cb = gla_chunk_scan(flip_seq(qg_c), flip_seq(kg_c), flip_seq(vg_c), flip_seq(lab_c), zero)
        gla_c = gla_output(o_cf + flip_seq(o_cb), gg_c, gla_out_norm_g)
    else:
        s_cf = gla_final_state(kg_c, vg_c, laf_c)
        s_cb = gla_final_state(flip_seq(kg_c), flip_seq(vg_c), flip_seq(lab_c))
    qg_l = to_heads(gq_l, GLA_HEADS) * q_scale
    kg_l, vg_l = to_heads(gk_l, GLA_HEADS), to_heads(gv_l, GLA_HEADS)
    _, o_lf = gla_chunk_scan(qg_l, kg_l, vg_l, log_decay(lrf_l, 0), s_cf)
    _, o_lb = gla_chunk_scan(flip_seq(qg_l), flip_seq(kg_l), flip_seq(vg_l),
                             flip_seq(log_decay(lrb_l, 1)), s_cb)
    gla_l = gla_output(o_lf + flip_seq(o_lb), gg_l, gla_out_norm_g)

    pool_l = pool_branch(pin_l, pool_w, pool_scale)
    y_lat = merge_branches(pool_l, mla_l, gla_l, gate_l, w_branch, w_out)
    if not need_ctx_out:
        return y_lat, None

    q_c = mla_queries(cq_c, mla_q_norm_g, mla_w_q_up, mla_q_head_g, None)
    mla_c = from_heads(block_attention(q_c, k_c, v_c))
    pool_c = pool_branch(pin_c, pool_w, pool_scale)
    y_ctx = merge_branches(pool_c, mla_c, gla_c, gate_c, w_branch, w_out)
    return y_lat, y_ctx


def setup_inputs(seed: int = 0) -> dict:
    key = jax.random.key(seed)
    ks = jax.random.split(key, 32)
    f32 = jnp.float32
    D, L = D_MODEL, DEPTH

    def nrm(k, shape, fan_in, gain=1.0):
        return gain * fan_in ** -0.5 * jax.random.normal(k, shape, f32)

    def ones_noise(k, shape):
        return 1.0 + 0.02 * jax.random.normal(k, shape, f32)

    return {
        "x": jax.random.normal(ks[0], (BATCH, SEQ, D), f32),
        "c": jax.random.normal(ks[1], (BATCH, D), f32),
        "ctx": jax.random.normal(ks[2], (BATCH, CTX_LEN, D), f32),
        "c_ctx": jax.random.normal(ks[3], (D,), f32),
        "w_mod": nrm(ks[4], (L, D, 6 * D), D, 0.5),
        "b_mod": 0.01 * jax.random.normal(ks[5], (L, 6 * D), f32),
        "norm1_g": ones_noise(ks[6], (L, D)),
        "norm2_g": ones_noise(ks[7], (L, D)),
        "w_in": nrm(ks[8], (L, D, IN_COLS), D),
        "mla_q_norm_g": ones_noise(ks[9], (L, MLA_Q_LORA)),
        "mla_w_q_up": nrm(ks[10], (L, MLA_Q_LORA, MLA_HEADS * MLA_QK), MLA_Q_LORA),
        "mla_kv_norm_g": ones_noise(ks[11], (L, MLA_KV_LORA)),
        "mla_w_kv_up": nrm(ks[12], (L, MLA_KV_LORA, MLA_HEADS * (MLA_NOPE + MLA_V)), MLA_KV_LORA),
        "mla_q_head_g": ones_noise(ks[13], (L, MLA_QK)),
        "mla_k_head_g": ones_noise(ks[14], (L, MLA_QK)),
        "gla_w_gate_up": nrm(ks[15], (L, 2, GLA_GATE_RANK, GLA_HEADS * GLA_DK), GLA_GATE_RANK),
        "gla_b_gate": 0.1 * jax.random.normal(ks[16], (L, 2, GLA_HEADS * GLA_DK), f32),
        "gla_out_norm_g": ones_noise(ks[17], (L, GLA_DV)),
        "pool_w": nrm(ks[18], (L, N_POOL_GROUPS, POOL_GROUP, POOL_GROUP), POOL_GROUP),
        "pool_scale": 1.0 + 0.1 * jax.random.normal(ks[19], (L, POOL_DIM), f32),
        "w_branch": nrm(ks[20], (L, N_BRANCH, BRANCH_DIM, D), BRANCH_DIM),
        "w_out": nrm(ks[21], (L, D, D), D),
        "ffn_w1": nrm(ks[22], (L, D, FFN_HIDDEN), D),
        "ffn_w3": nrm(ks[23], (L, D, FFN_HIDDEN), D),
        "ffn_w2": nrm(ks[24], (L, FFN_HIDDEN, D), FFN_HIDDEN),
    }


def reference(x, c, ctx, c_ctx, w_mod, b_mod, norm1_g, norm2_g, w_in, mla_q_norm_g, mla_w_q_up,
              mla_kv_norm_g, mla_w_kv_up, mla_q_head_g, mla_k_head_g, gla_w_gate_up, gla_b_gate,
              gla_out_norm_g, pool_w, pool_scale, w_branch, w_out, ffn_w1, ffn_w3, ffn_w2):
    ROWS = x.shape[1] // GRID_W
    rope = axial_rope_tables(ROWS, x.dtype)
    silu_c = jax.nn.silu(c)
    silu_cc = jax.nn.silu(c_ctx)
    xc = ctx
    for l in range(DEPTH):
        last = l == DEPTH - 1
        mod = silu_c @ w_mod[l] + b_mod[l]
        sh1, sc1, g1, sh2, sc2, g2 = jnp.split(mod[:, None, :], 6, axis=-1)
        n_mod = 2 if last else 6
        mods_c = jnp.split(silu_cc @ w_mod[l][:, :n_mod * D_MODEL] + b_mod[l][:n_mod * D_MODEL], n_mod)

        h_lat = modulate(rms_norm(x, norm1_g[l]), sh1, sc1)
        h_ctx = modulate(rms_norm(xc, norm1_g[l]), mods_c[0], mods_c[1])
        y_lat, y_ctx = hybrid_mixer(h_lat, h_ctx, not last, rope, w_in[l], mla_q_norm_g[l], mla_w_q_up[l],
                                    mla_kv_norm_g[l], mla_w_kv_up[l], mla_q_head_g[l], mla_k_head_g[l],
                                    gla_w_gate_up[l], gla_b_gate[l], gla_out_norm_g[l], pool_w[l],
                                    pool_scale[l], w_branch[l], w_out[l])
        x = x + g1 * y_lat
        x = x + g2 * swiglu(modulate(rms_norm(x, norm2_g[l]), sh2, sc2), ffn_w1[l], ffn_w3[l], ffn_w2[l])
        if not last:
            xc = xc + mods_c[2] * y_ctx
            xc = xc + mods_c[5] * swiglu(modulate(rms_norm(xc, norm2_g[l]), mods_c[3], mods_c[4]),
                                         ffn_w1[l], ffn_w3[l], ffn_w2[l])
    return x
```

```python
import functools

import numpy as np
import jax
import jax.numpy as jnp
from jax import lax
from jax.experimental import pallas as pl
from jax.experimental.pallas import tpu as pltpu

F32 = jnp.float32
BF16 = jnp.bfloat16

GRID_W = 64
RMS_EPS = 1e-6
N_BRANCH = 3
BRANCH_DIM = 1024
POOL_WINDOWS = (2, 4, 8, 16)
N_POOL_GROUPS = len(POOL_WINDOWS)
POOL_GROUP = BRANCH_DIM // N_POOL_GROUPS
MLA_HEADS = 8
MLA_LORA = 512
MLA_NOPE = 128
MLA_ROPE = 64
MLA_V = BRANCH_DIM // MLA_HEADS
MLA_QK = MLA_NOPE + MLA_ROPE
ROPE_BASE = 10000.0
GLA_HEADS = 4
GLA_DK = 256
GLA_DV = BRANCH_DIM // GLA_HEADS
GLA_GATE_RANK = 16
GLA_GATE_TAU = 16.0
N_MOD = 6

V7X_VMEM_BYTES = 64 * 1024 * 1024
V7X_LANES = 128
MXU_DIM = 256

MLA_HEAD_PAD = 2 * V7X_LANES
GLA_CHUNK = 64
GLA_SUB = 16
POOL_HALO = 16
TAIL_COLS = V7X_LANES
TAIL_LRF = MLA_ROPE
TAIL_LRB = MLA_ROPE + GLA_GATE_RANK

Z_CKV = 0
Z_CQ = Z_CKV + MLA_LORA
Z_GK = Z_CQ + MLA_LORA
Z_GV = Z_GK + GLA_HEADS * GLA_DK
Z_GQ = Z_GV + GLA_HEADS * GLA_DV
Z_GG = Z_GQ + GLA_HEADS * GLA_DK
Z_PIN = Z_GG + GLA_HEADS * GLA_DV
Z_GATE = Z_PIN + BRANCH_DIM
Z_KV_END = Z_GQ


def _divisor_tile(n, pref, align):
    if n <= pref:
        return n
    best = None
    for t in range(align, pref + 1, align):
        if n % t == 0:
            best = t
    assert best is not None, (n, pref, align)
    return best


def _params(semantics, vmem_bytes):
    limit = int(min(V7X_VMEM_BYTES - 6 * 1024 * 1024, max(vmem_bytes, 16 * 1024 * 1024)))
    return pltpu.CompilerParams(dimension_semantics=semantics, vmem_limit_bytes=limit)


def _nbytes(shape, dtype):
    return int(np.prod(shape)) * jnp.dtype(dtype).itemsize


def _sigmoid(x):
    return 1.0 / (1.0 + jnp.exp(-x))


def _k_mods(c_ref, w_ref, b_ref, o_ref):
    cv = c_ref[...]
    a = cv * _sigmoid(cv)
    o_ref[...] = jnp.dot(a, w_ref[...], precision=lax.Precision.HIGHEST,
                         preferred_element_type=F32) + b_ref[...]


def _mods_call(c8, w_mod, b_mod):
    L, D, N = w_mod.shape
    tn = _divisor_tile(N, 2048, V7X_LANES)
    vm = 2 * _nbytes((D, tn), F32) + 4 * _nbytes((8, tn), F32) + 2 * _nbytes((8, D), F32) + (4 << 20)
    return pl.pallas_call(
        _k_mods,
        grid=(L, N // tn),
        in_specs=[pl.BlockSpec((8, D), lambda l, j: (0, 0)),
                  pl.BlockSpec((None, D, tn), lambda l, j: (l, 0, j)),
                  pl.BlockSpec((None, 1, tn), lambda l, j: (l, 0, j))],
        out_specs=pl.BlockSpec((None, 8, tn), lambda l, j: (l, 0, j)),
        out_shape=jax.ShapeDtypeStruct((L, 8, N), F32),
        compiler_params=_params(("arbitrary", "arbitrary"), vm),
        name="mods",
    )(c8, w_mod, b_mod.reshape(L, 1, N))


NORM_ROWS = 256


def _norm_mod_rows(x_ref, g_ref, sh_ref, sc_ref, h_ref):
    tm = x_ref.shape[0]
    step = min(NORM_ROWS, tm)
    g = g_ref[...]
    sh = sh_ref[...]
    sc1 = 1.0 + sc_ref[...]

    def body(r, carry):
        rows = pl.ds(pl.multiple_of(r * step, step), step)
        x = x_ref[rows, :]
        ms = jnp.mean(x * x, axis=-1, keepdims=True)
        y = x * lax.rsqrt(ms + RMS_EPS) * g
        h_ref[rows, :] = (y * sc1 + sh).astype(h_ref.dtype)
        return carry

    lax.fori_loop(0, tm // step, body, 0)


def _k_norm_matmul(x_ref, g_ref, sh_ref, sc_ref, w_ref, o_ref, h_scr):
    @pl.when(pl.program_id(1) == 0)
    def _():
        _norm_mod_rows(x_ref, g_ref, sh_ref, sc_ref, h_scr)

    o_ref[...] = jnp.dot(h_scr[...], w_ref[...], preferred_element_type=F32).astype(o_ref.dtype)


def _k_norm_swiglu_up(x_ref, g_ref, sh_ref, sc_ref, w1_ref, w3_ref, o_ref, h_scr):
    @pl.when(pl.program_id(1) == 0)
    def _():
        _norm_mod_rows(x_ref, g_ref, sh_ref, sc_ref, h_scr)

    h = h_scr[...]
    a = jnp.dot(h, w1_ref[...], preferred_element_type=F32)
    b = jnp.dot(h, w3_ref[...], preferred_element_type=F32)
    o_ref[...] = (a * _sigmoid(a) * b).astype(o_ref.dtype)


def _k_matmul_residual(a_ref, w_ref, x_ref, gate_ref, o_ref):
    y = jnp.dot(a_ref[...], w_ref[...], preferred_element_type=F32)
    o_ref[...] = x_ref[...] + gate_ref[...] * y


def _k_merge(p_ref, m_ref, gl_ref, g0_ref, g1_ref, g2_ref, wb_ref, o_ref):
    acc = None
    for n, (br, gt) in enumerate(((p_ref, g0_ref), (m_ref, g1_ref), (gl_ref, g2_ref))):
        proj = jnp.dot(br[...], wb_ref[n], preferred_element_type=F32)
        term = _sigmoid(gt[...].astype(F32)) * proj
        acc = term if acc is None else acc + term
    o_ref[...] = acc.astype(o_ref.dtype)


class _Tokens:
    def __init__(self, n_rows, rows_per_mod, mod_row_base, tm_pref=1024):
        self.m = n_rows
        if rows_per_mod is None:
            self.tm = _divisor_tile(n_rows, tm_pref, 8)
            self.mod_row = lambda i: mod_row_base
        else:
            self.tm = _divisor_tile(rows_per_mod, tm_pref, 8)
            per = rows_per_mod // self.tm
            self.mod_row = lambda i: mod_row_base + i // per
        self.n_tiles = n_rows // self.tm

    def mod_spec(self, k, d, tn=None):
        if tn is None:
            return pl.BlockSpec((None, 1, d), lambda i, j: (self.mod_row(i) * N_MOD + k, 0, 0))
        return pl.BlockSpec((None, 1, tn), lambda i, j: (self.mod_row(i) * N_MOD + k, 0, j))


def _norm_matmul_call(tok, x, g, mods, k_shift, w, n_cols, tn_pref, out_dtype, name):
    m, d = x.shape
    tm = tok.tm
    tn = _divisor_tile(n_cols, tn_pref, V7X_LANES)
    vm = (2 * _nbytes((tm, d), F32) + _nbytes((tm, d), BF16) + 2 * _nbytes((d, tn), BF16)
          + 2 * _nbytes((tm, tn), out_dtype) + 2 * _nbytes((tm, tn), F32) + (4 << 20))
    return pl.pallas_call(
        _k_norm_matmul,
        grid=(tok.n_tiles, n_cols // tn),
        in_specs=[pl.BlockSpec((tm, d), lambda i, j: (i, 0)),
                  pl.BlockSpec((1, d), lambda i, j: (0, 0)),
                  tok.mod_spec(k_shift, d),
                  tok.mod_spec(k_shift + 1, d),
                  pl.BlockSpec((d, tn), lambda i, j: (0, j))],
        out_specs=pl.BlockSpec((tm, tn), lambda i, j: (i, j)),
        out_shape=jax.ShapeDtypeStruct((m, n_cols), out_dtype),
        scratch_shapes=[pltpu.VMEM((tm, d), BF16)],
        compiler_params=_params(("arbitrary", "arbitrary"), vm),
        name=name,
    )(x, g, mods, mods, w)


def _swiglu_up_call(tok, x, g, mods, w1, w3):
    m, d = x.shape
    f = w1.shape[1]
    tm = tok.tm
    tn = _divisor_tile(f, 512, V7X_LANES)
    vm = (2 * _nbytes((tm, d), F32) + _nbytes((tm, d), BF16) + 4 * _nbytes((d, tn), BF16)
          + 2 * _nbytes((tm, tn), BF16) + 4 * _nbytes((tm, tn), F32) + (4 << 20))
    return pl.pallas_call(
        _k_norm_swiglu_up,
        grid=(tok.n_tiles, f // tn),
        in_specs=[pl.BlockSpec((tm, d), lambda i, j: (i, 0)),
                  pl.BlockSpec((1, d), lambda i, j: (0, 0)),
                  tok.mod_spec(3, d),
                  tok.mod_spec(4, d),
                  pl.BlockSpec((d, tn), lambda i, j: (0, j)),
                  pl.BlockSpec((d, tn), lambda i, j: (0, j))],
        out_specs=pl.BlockSpec((tm, tn), lambda i, j: (i, j)),
        out_shape=jax.ShapeDtypeStruct((m, f), BF16),
        scratch_shapes=[pltpu.VMEM((tm, d), BF16)],
        compiler_params=_params(("arbitrary", "arbitrary"), vm),
        name="swiglu_up",
    )(x, g, mods, mods, w1, w3)


def _matmul_residual_call(tok, a, w, x, mods, k_gate, tn_pref, name):
    m, kdim = a.shape
    d = w.shape[1]
    tm = tok.tm
    tn = _divisor_tile(d, tn_pref, V7X_LANES)
    vm = (2 * _nbytes((tm, kdim), BF16) + 2 * _nbytes((kdim, tn), BF16)
          + 5 * _nbytes((tm, tn), F32) + (4 << 20))
    return pl.pallas_call(
        _k_matmul_residual,
        grid=(tok.n_tiles, d // tn),
        in_specs=[pl.BlockSpec((tm, kdim), lambda i, j: (i, 0)),
                  pl.BlockSpec((kdim, tn), lambda i, j: (0, j)),
                  pl.BlockSpec((tm, tn), lambda i, j: (i, j)),
                  tok.mod_spec(k_gate, d, tn)],
        out_specs=pl.BlockSpec((tm, tn), lambda i, j: (i, j)),
        out_shape=jax.ShapeDtypeStruct((m, d), F32),
        compiler_params=_params(("arbitrary", "arbitrary"), vm),
        name=name,
    )(a, w, x, mods)


def _merge_call(tok, pool_o, mla_o, gla_o, z, w_branch, d):
    m = pool_o.shape[0]
    tm = tok.tm
    tn = _divisor_tile(d, 512, V7X_LANES)
    assert Z_GATE % tn == 0
    g0 = Z_GATE // tn
    per = d // tn
    br_spec = pl.BlockSpec((tm, BRANCH_DIM), lambda i, j: (i, 0))
    vm = (6 * _nbytes((tm, BRANCH_DIM), BF16) + 6 * _nbytes((tm, tn), BF16)
          + 2 * _nbytes((N_BRANCH, BRANCH_DIM, tn), BF16) + 2 * _nbytes((tm, tn), BF16)
          + 4 * _nbytes((tm, tn), F32) + (4 << 20))
    return pl.pallas_call(
        _k_merge,
        grid=(tok.n_tiles, per),
        in_specs=[br_spec, br_spec, br_spec,
                  pl.BlockSpec((tm, tn), lambda i, j: (i, g0 + j)),
                  pl.BlockSpec((tm, tn), lambda i, j: (i, g0 + per + j)),
                  pl.BlockSpec((tm, tn), lambda i, j: (i, g0 + 2 * per + j)),
                  pl.BlockSpec((N_BRANCH, BRANCH_DIM, tn), lambda i, j: (0, 0, j))],
        out_specs=pl.BlockSpec((tm, tn), lambda i, j: (i, j)),
        out_shape=jax.ShapeDtypeStruct((m, d), BF16),
        compiler_params=_params(("arbitrary", "arbitrary"), vm),
        name="merge",
    )(pool_o, mla_o, gla_o, z, z, z, w_branch)


def _k_mla_prep(use_rope, zkv_ref, zq_ref, zt_ref, wq_ref, wkv_ref, qng_ref, kvng_ref,
                qgn_ref, qgr_ref, kgn_ref, kgr_ref, *rest):
    if use_rope:
        c_ref, sa_ref, sb_ref, q_out, k_out, v_out = rest
        rc, rsa, rsb = c_ref[...], sa_ref[...], sb_ref[...]
    else:
        q_out, k_out, v_out = rest
    hn = MLA_HEADS * MLA_NOPE

    def rms(x, g):
        xf = x.astype(F32)
        return xf * lax.rsqrt(jnp.mean(xf * xf, axis=-1, keepdims=True) + RMS_EPS) * g

    def rope(x):
        if not use_rope:
            return x
        return (x * rc + pltpu.roll(x, V7X_LANES - MLA_ROPE // 4, 1) * rsa
                + pltpu.roll(x, MLA_ROPE // 4, 1) * rsb)

    q = jnp.dot(rms(zq_ref[...], qng_ref[...]).astype(BF16), wq_ref[...], preferred_element_type=F32)
    kv = jnp.dot(rms(zkv_ref[...], kvng_ref[...]).astype(BF16), wkv_ref[...], preferred_element_type=F32)
    v_out[...] = kv[:, hn:].astype(v_out.dtype)

    zt = zt_ref[...]
    lane = lax.broadcasted_iota(jnp.int32, zt.shape, 1)
    kr = jnp.where(lane < MLA_ROPE, zt, 0.0)
    kr_ss = jnp.sum(kr * kr, axis=-1, keepdims=True)
    qgn, qgr, kgn, kgr = qgn_ref[...], qgr_ref[...], kgn_ref[...], kgr_ref[...]
    inv_dim = 1.0 / MLA_QK
    q_scale = MLA_QK ** -0.5
    for h in range(MLA_HEADS):
        lo = h * MLA_HEAD_PAD
        qn = q[:, h * MLA_NOPE:(h + 1) * MLA_NOPE]
        qr = q[:, hn + h * V7X_LANES: hn + (h + 1) * V7X_LANES]
        ss = jnp.sum(qn * qn, axis=-1, keepdims=True) + jnp.sum(qr * qr, axis=-1, keepdims=True)
        r = lax.rsqrt(ss * inv_dim + RMS_EPS) * q_scale
        q_out[:, lo:lo + MLA_NOPE] = (qn * r * qgn).astype(q_out.dtype)
        q_out[:, lo + MLA_NOPE:lo + MLA_HEAD_PAD] = rope(qr * r * qgr).astype(q_out.dtype)
        kn = kv[:, h * MLA_NOPE:(h + 1) * MLA_NOPE]
        rk = lax.rsqrt((jnp.sum(kn * kn, axis=-1, keepdims=True) + kr_ss) * inv_dim + RMS_EPS)
        k_out[:, lo:lo + MLA_NOPE] = (kn * rk * kgn).astype(k_out.dtype)
        k_out[:, lo + MLA_NOPE:lo + MLA_HEAD_PAD] = rope(kr * rk * kgr).astype(k_out.dtype)


def _mla_prep_call(z, zt, wq, wkv, gains, rope_tabs, seq_len):
    m = z.shape[0]
    tm = _divisor_tile(seq_len, 256, 16)
    use_rope = rope_tabs is not None
    hq = MLA_HEADS * MLA_HEAD_PAD
    hv = MLA_HEADS * MLA_V
    full = lambda shape: pl.BlockSpec(shape, lambda i: tuple(0 for _ in shape))
    in_specs = [pl.BlockSpec((tm, MLA_LORA), lambda i: (i, Z_CKV // MLA_LORA)),
                pl.BlockSpec((tm, MLA_LORA), lambda i: (i, Z_CQ // MLA_LORA)),
                pl.BlockSpec((tm, TAIL_COLS), lambda i: (i, 0)),
                full(wq.shape), full(wkv.shape)]
    in_specs += [full(g.shape) for g in gains]
    args = [z, z, zt, wq, wkv, *gains]
    if use_rope:
        per = seq_len // tm
        in_specs += [pl.BlockSpec((tm, V7X_LANES), lambda i: (i % per, 0))] * 3
        args += list(rope_tabs)
    vm = (8 * _nbytes((tm, hq), F32) + 4 * _nbytes(wq.shape, BF16) + (8 << 20))
    return pl.pallas_call(
        functools.partial(_k_mla_prep, use_rope),
        grid=(m // tm,),
        in_specs=in_specs,
        out_specs=[pl.BlockSpec((tm, hq), lambda i: (i, 0)),
                   pl.BlockSpec((tm, hq), lambda i: (i, 0)),
                   pl.BlockSpec((tm, hv), lambda i: (i, 0))],
        out_shape=[jax.ShapeDtypeStruct((m, hq), BF16),
                   jax.ShapeDtypeStruct((m, hq), BF16),
                   jax.ShapeDtypeStruct((m, hv), BF16)],
        compiler_params=_params(("arbitrary",), vm),
        name="mla_prep_rope" if use_rope else "mla_prep",
    )(*args)


_NT = (((1,), (1,)), ((), ()))
_TN = (((0,), (0,)), ((), ()))


def _k_attn_lat(q_ref, kl_ref, kc_ref, vl_ref, vc_ref, o_ref):
    q = q_ref[...]
    sl = lax.dot_general(q, kl_ref[...], _NT, preferred_element_type=F32)
    sc = lax.dot_general(q, kc_ref[...], _NT, preferred_element_type=F32)
    mx = jnp.maximum(jnp.max(sl, axis=-1, keepdims=True), jnp.max(sc, axis=-1, keepdims=True))
    p_l = jnp.exp(sl - mx)
    p_c = jnp.exp(sc - mx)
    den = jnp.sum(p_l, axis=-1, keepdims=True) + jnp.sum(p_c, axis=-1, keepdims=True)
    o = (jnp.dot(p_c.astype(BF16), vc_ref[...], preferred_element_type=F32)
         + jnp.dot(p_l.astype(BF16), vl_ref[...], preferred_element_type=F32))
    o_ref[...] = (o / den).astype(o_ref.dtype)


def _k_attn_ctx(q_ref, kc_ref, vc_ref, o_ref):
    sc = lax.dot_general(q_ref[...], kc_ref[...], _NT, preferred_element_type=F32)
    p_c = jnp.exp(sc - jnp.max(sc, axis=-1, keepdims=True))
    den = jnp.sum(p_c, axis=-1, keepdims=True)
    o = jnp.dot(p_c.astype(BF16), vc_ref[...], preferred_element_type=F32)
    o_ref[...] = (o / den).astype(o_ref.dtype)


def _attn_lat_call(q_l, k_l, v_l, k_c, v_c, batch, seq, ctx):
    tq = _divisor_tile(seq, 256, 16)
    nq = seq // tq
    vm = (4 * _nbytes((tq, seq + ctx), F32) + 2 * _nbytes((seq + ctx, MLA_HEAD_PAD + MLA_V), BF16)
          + (8 << 20))
    return pl.pallas_call(
        _k_attn_lat,
        grid=(batch, MLA_HEADS, nq),
        in_specs=[pl.BlockSpec((tq, MLA_HEAD_PAD), lambda b, h, i: (b * nq + i, h)),
                  pl.BlockSpec((seq, MLA_HEAD_PAD), lambda b, h, i: (b, h)),
                  pl.BlockSpec((ctx, MLA_HEAD_PAD), lambda b, h, i: (b, h)),
                  pl.BlockSpec((seq, MLA_V), lambda b, h, i: (b, h)),
                  pl.BlockSpec((ctx, MLA_V), lambda b, h, i: (b, h))],
        out_specs=pl.BlockSpec((tq, MLA_V), lambda b, h, i: (b * nq + i, h)),
        out_shape=jax.ShapeDtypeStruct((batch * seq, MLA_HEADS * MLA_V), BF16),
        compiler_params=_params(("arbitrary", "arbitrary", "arbitrary"), vm),
        name="attn_lat",
    )(q_l, k_l, k_c, v_l, v_c)


def _attn_ctx_call(q_c, k_c, v_c, batch, ctx):
    vm = 4 * _nbytes((ctx, ctx), F32) + (8 << 20)
    return pl.pallas_call(
        _k_attn_ctx,
        grid=(batch, MLA_HEADS),
        in_specs=[pl.BlockSpec((ctx, MLA_HEAD_PAD), lambda b, h: (b, h)),
                  pl.BlockSpec((ctx, MLA_HEAD_PAD), lambda b, h: (b, h)),
                  pl.BlockSpec((ctx, MLA_V), lambda b, h: (b, h))],
        out_specs=pl.BlockSpec((ctx, MLA_V), lambda b, h: (b, h)),
        out_shape=jax.ShapeDtypeStruct((batch * ctx, MLA_HEADS * MLA_V), BF16),
        compiler_params=_params(("arbitrary", "arbitrary"), vm),
        name="attn_ctx",
    )(q_c, k_c, v_c)


def _gla_constants():
    c, sub = GLA_CHUNK, GLA_SUB
    t = np.arange(c)[:, None]
    s = np.arange(c)[None, :]
    half, quarter = c // 2, c // 4
    assert quarter == sub
    sums, masks = [], []
    for d in range(2):
        if d == 0:
            before = lambda idx: s <= idx
            ref1 = np.full_like(t, half - 1)
            ref2 = np.where(t < half, quarter - 1, half + quarter - 1)
            ref3 = (t // sub) * sub - 1
            m1 = (t >= half) & (s < half)
            m2 = (((t // sub) % 2) == 1) & (s // sub == t // sub - 1)
            m3 = (t // sub == s // sub) & (s <= t)
        else:
            before = lambda idx: s >= idx
            ref1 = np.full_like(t, half)
            ref2 = np.where(t < half, quarter, half + quarter)
            ref3 = (t // sub + 1) * sub
            m1 = (t < half) & (s >= half)
            m2 = (((t // sub) % 2) == 0) & (s // sub == t // sub + 1)
            m3 = (t // sub == s // sub) & (s >= t)
        sums.append(np.concatenate([before(t), before(ref1), before(ref2), before(ref3)], 0))
        masks.append(np.stack([m1, m2, m3], 0))
    return (jnp.asarray(np.stack(sums, 0), BF16), jnp.asarray(np.stack(masks, 0), F32))


def _k_gla(need_ctx_out, ql_ref, kl_ref, vl_ref, ggl_ref, ztl_ref, kc_ref, vc_ref, ztc_ref,
           wg_ref, bg_ref, gn_ref, sums_ref, masks_ref, *rest):
    if need_ctx_out:
        qc_ref, ggc_ref, ol_ref, oc_ref, acc_l, acc_c, st_ref = rest
    else:
        ol_ref, acc_l, st_ref = rest
        qc_ref = ggc_ref = oc_ref = acc_c = None
    c = GLA_CHUNK
    q_scale = GLA_DK ** -0.5
    inv_tau = 1.0 / GLA_GATE_TAU

    def chunk(d, q_ref, k_ref, v_ref, zt_ref, j, acc_ref):
        rows = pl.ds(pl.multiple_of(j * c, c), c)
        k = k_ref[rows, :].astype(F32)
        v = v_ref[rows, :]
        zg = jnp.dot(zt_ref[rows, :], wg_ref[d], precision=lax.Precision.HIGHEST,
                     preferred_element_type=F32) + bg_ref[d]
        la = (jnp.minimum(zg, 0.0) - jnp.log1p(jnp.exp(-jnp.abs(zg)))) * inv_tau
        hi = la.astype(BF16)
        lo = (la - hi.astype(F32)).astype(BF16)
        sm = sums_ref[d]
        c4 = (jnp.dot(sm, hi, preferred_element_type=F32) + jnp.dot(sm, lo, preferred_element_type=F32))
        cum = c4[0:c]
        tot = cum[c - 1:c] if d == 0 else cum[0:1]
        st = st_ref[d]
        if acc_ref is not None:
            q = q_ref[rows, :].astype(F32) * q_scale
            inter = lax.dot_general((q * jnp.exp(cum)).astype(BF16), st.astype(BF16), _NT,
                                    preferred_element_type=F32)
            a = None
            for lvl in range(3):
                ref = c4[(lvl + 1) * c:(lvl + 2) * c]
                dq = cum - ref
                dk = ref - cum
                if lvl < 2:
                    dq = jnp.minimum(dq, 0.0)
                    dk = jnp.minimum(dk, 0.0)
                qe = (q * jnp.exp(dq)).astype(BF16)
                ke = (k * jnp.exp(dk)).astype(BF16)
                part = masks_ref[d, lvl] * lax.dot_general(qe, ke, _NT, preferred_element_type=F32)
                a = part if a is None else a + part
            intra = jnp.dot(a.astype(BF16), v, preferred_element_type=F32)
            acc_ref[rows, :] += inter + intra
        kd = (k * jnp.exp(tot - cum)).astype(BF16)
        st_ref[d] = st * jnp.exp(tot) + lax.dot_general(v, kd, _TN, preferred_element_type=F32)

    def scan(q_ref, k_ref, v_ref, zt_ref, acc_ref):
        n = k_ref.shape[0] // c
        if acc_ref is not None:
            acc_ref[...] = jnp.zeros(acc_ref.shape, acc_ref.dtype)

        def body(i, carry):
            chunk(0, q_ref, k_ref, v_ref, zt_ref, i, acc_ref)
            chunk(1, q_ref, k_ref, v_ref, zt_ref, n - 1 - i, acc_ref)
            return carry

        lax.fori_loop(0, n, body, 0)

    def finish(acc_ref, gg_ref, o_ref):
        n_rows = acc_ref.shape[0]
        step = min(NORM_ROWS, n_rows)
        gn = gn_ref[...]

        def body(r, carry):
            rows = pl.ds(pl.multiple_of(r * step, step), step)
            o = acc_ref[rows, :]
            y = o * lax.rsqrt(jnp.mean(o * o, axis=-1, keepdims=True) + RMS_EPS) * gn
            g = gg_ref[rows, :].astype(F32)
            o_ref[rows, :] = (y * (g * _sigmoid(g))).astype(o_ref.dtype)
            return carry

        lax.fori_loop(0, n_rows // step, body, 0)

    st_ref[...] = jnp.zeros(st_ref.shape, st_ref.dtype)
    scan(qc_ref, kc_ref, vc_ref, ztc_ref, acc_c)
    scan(ql_ref, kl_ref, vl_ref, ztl_ref, acc_l)
    finish(acc_l, ggl_ref, ol_ref)
    if need_ctx_out:
        finish(acc_c, ggc_ref, oc_ref)


def _gla_call(z_l, zt_l, z_c, zt_c, wg, bg, gn, consts, batch, seq, ctx, need_ctx_out):
    sums, masks = consts
    blk = lambda rows, col0: pl.BlockSpec((rows, GLA_DK), lambda b, h: (b, col0 // GLA_DK + h))
    tail = lambda rows: pl.BlockSpec((rows, TAIL_COLS), lambda b, h: (b, 0))
    full = lambda a: pl.BlockSpec(a.shape, lambda b, h: tuple(0 for _ in a.shape))
    in_specs = [blk(seq, Z_GQ), blk(seq, Z_GK), blk(seq, Z_GV), blk(seq, Z_GG), tail(seq),
                blk(ctx, Z_GK), blk(ctx, Z_GV), tail(ctx),
                pl.BlockSpec((2, TAIL_COLS, GLA_DK), lambda b, h: (0, 0, h)),
                pl.BlockSpec((2, 1, GLA_DK), lambda b, h: (0, 0, h)),
                full(gn), full(sums), full(masks)]
    args = [z_l, z_l, z_l, z_l, zt_l, z_c, z_c, zt_c, wg, bg, gn, sums, masks]
    out_specs = [pl.BlockSpec((seq, GLA_DV), lambda b, h: (b, h))]
    out_shape = [jax.ShapeDtypeStruct((batch * seq, GLA_HEADS * GLA_DV), BF16)]
    scratch = [pltpu.VMEM((seq, GLA_DV), F32)]
    if need_ctx_out:
        in_specs += [blk(ctx, Z_GQ), blk(ctx, Z_GG)]
        args += [z_c, z_c]
        out_specs.append(pl.BlockSpec((ctx, GLA_DV), lambda b, h: (b, h)))
        out_shape.append(jax.ShapeDtypeStruct((batch * ctx, GLA_HEADS * GLA_DV), BF16))
        scratch.append(pltpu.VMEM((ctx, GLA_DV), F32))
    scratch.append(pltpu.VMEM((2, GLA_DV, GLA_DK), F32))
    vm = (10 * _nbytes((seq + ctx, GLA_DK), BF16) + 2 * _nbytes((seq + ctx, TAIL_COLS), F32)
          + _nbytes((seq + ctx, GLA_DV), F32) + (12 << 20))
    outs = pl.pallas_call(
        functools.partial(_k_gla, need_ctx_out),
        grid=(batch, GLA_HEADS),
        in_specs=in_specs,
        out_specs=out_specs,
        out_shape=out_shape,
        scratch_shapes=scratch,
        compiler_params=_params(("arbitrary", "arbitrary"), vm),
        name="gla" if need_ctx_out else "gla_last",
    )(*args)
    return (outs[0], outs[1]) if need_ctx_out else (outs[0], None)


POOL_ROWS = 256


def _pool_bands():
    t = np.arange(POOL_ROWS)[:, None]
    m = np.arange(POOL_ROWS + 2 * POOL_HALO)[None, :] - POOL_HALO
    bands = [(m >= t - w // 2) & (m <= t + w // 2 - 1) for w in POOL_WINDOWS]
    return jnp.asarray(np.stack(bands, 0), BF16)


def _k_pool(ul_ref, uc_ref, band_ref, w_ref, sc_ref, ol_ref, oc_ref, padl_ref, padc_ref):
    half = lax.shift_left(jnp.int32(1), pl.program_id(1))
    w = w_ref[...]
    scale = sc_ref[...]

    def run(u_ref, pad_ref, o_ref):
        n = u_ref.shape[0]
        rows = min(POOL_ROWS, n)
        zeros = jnp.zeros((POOL_HALO, pad_ref.shape[1]), pad_ref.dtype)
        pad_ref[0:POOL_HALO, :] = zeros
        pad_ref[POOL_HALO + n:2 * POOL_HALO + n, :] = zeros
        pad_ref[POOL_HALO:POOL_HALO + n, :] = u_ref[...]
        band = band_ref[0:rows, 0:rows + 2 * POOL_HALO]

        def body(r, carry):
            r0 = pl.multiple_of(r * rows, rows)
            win = jnp.dot(band, pad_ref[pl.ds(r0, rows + 2 * POOL_HALO), :],
                          preferred_element_type=F32)
            pos = r0 + lax.broadcasted_iota(jnp.int32, (rows, 1), 0)
            cnt = (jnp.minimum(pos, half) + jnp.minimum(n - pos, half)).astype(F32)
            pooled = win / cnt - u_ref[pl.ds(r0, rows), :].astype(F32)
            y = jnp.dot(pooled.astype(BF16), w, preferred_element_type=F32) * scale
            o_ref[pl.ds(r0, rows), :] = y.astype(o_ref.dtype)
            return carry

        lax.fori_loop(0, n // rows, body, 0)

    run(ul_ref, padl_ref, ol_ref)
    if oc_ref is not None:
        run(uc_ref, padc_ref, oc_ref)


def _k_pool_lat_only(ul_ref, band_ref, w_ref, sc_ref, ol_ref, padl_ref):
    _k_pool(ul_ref, None, band_ref, w_ref, sc_ref, ol_ref, None, padl_ref, None)


def _pool_call(z_l, z_c, bands, pool_w, pool_scale, batch, seq, ctx):
    g0 = Z_PIN // POOL_GROUP
    blk = lambda rows: pl.BlockSpec((rows, POOL_GROUP), lambda b, g: (b, g0 + g))
    oblk = lambda rows: pl.BlockSpec((rows, POOL_GROUP), lambda b, g: (b, g))
    common = [pl.BlockSpec((None,) + bands.shape[1:], lambda b, g: (g, 0, 0)),
              pl.BlockSpec((None, POOL_GROUP, POOL_GROUP), lambda b, g: (g, 0, 0)),
              pl.BlockSpec((1, POOL_GROUP), lambda b, g: (0, g))]
    vm = 8 * _nbytes((seq + ctx, POOL_GROUP), F32) + (8 << 20)
    pad = lambda rows: pltpu.VMEM((rows + 2 * POOL_HALO, POOL_GROUP), BF16)
    if z_c is None:
        out = pl.pallas_call(
            _k_pool_lat_only, grid=(batch, N_POOL_GROUPS),
            in_specs=[blk(seq)] + common, out_specs=oblk(seq),
            out_shape=jax.ShapeDtypeStruct((batch * seq, BRANCH_DIM), BF16),
            scratch_shapes=[pad(seq)],
            compiler_params=_params(("arbitrary", "arbitrary"), vm), name="pool_last",
        )(z_l, bands, pool_w, pool_scale)
        return out, None
    return pl.pallas_call(
        _k_pool, grid=(batch, N_POOL_GROUPS),
        in_specs=[blk(seq), blk(ctx)] + common, out_specs=[oblk(seq), oblk(ctx)],
        out_shape=[jax.ShapeDtypeStruct((batch * seq, BRANCH_DIM), BF16),
                   jax.ShapeDtypeStruct((batch * ctx, BRANCH_DIM), BF16)],
        scratch_shapes=[pad(seq), pad(ctx)],
        compiler_params=_params(("arbitrary", "arbitrary"), vm), name="pool",
    )(z_l, z_c, bands, pool_w, pool_scale)


def _rope_tables(seq):
    half = MLA_ROPE // 2
    quarter = half // 2
    inv_freq = 1.0 / (ROPE_BASE ** (jnp.arange(0, half, 2, dtype=F32) / half))
    t = jnp.arange(seq)
    ang_r = (t // GRID_W).astype(F32)[:, None] * inv_freq[None, :]
    ang_c = (t % GRID_W).astype(F32)[:, None] * inv_freq[None, :]
    cr, sr, cc, sn = jnp.cos(ang_r), jnp.sin(ang_r), jnp.cos(ang_c), jnp.sin(ang_c)
    z = jnp.zeros((seq, quarter), F32)
    zpad = jnp.zeros((seq, V7X_LANES - MLA_ROPE), F32)
    cos = jnp.concatenate([cr, cr, cc, cc, zpad], axis=1)
    sa = jnp.concatenate([-sr, z, -sn, z, zpad], axis=1)
    sb = jnp.concatenate([z, sr, z, sn, zpad], axis=1)
    return cos, sa, sb


def _prep_weights(w_in, mla_w_q_up, mla_w_kv_up, mla_q_head_g, mla_k_head_g, gla_w_gate_up):
    L = w_in.shape[0]
    o = np.cumsum((0, MLA_LORA, MLA_ROPE, GLA_HEADS * GLA_DK, GLA_HEADS * GLA_DV, GLA_GATE_RANK,
                   GLA_GATE_RANK, MLA_LORA, GLA_HEADS * GLA_DK, GLA_HEADS * GLA_DV, BRANCH_DIM))
    ckv, kr, gk, gv, lrf, lrb, cq, gq, gg, pin, gate = (slice(int(a), int(b)) for a, b in
                                                        zip(o, list(o[1:]) + [w_in.shape[2]]))
    w_main = jnp.concatenate([w_in[..., s] for s in (ckv, cq, gk, gv, gq, gg, pin, gate)],
                             axis=-1).astype(BF16)
    d = w_in.shape[1]
    w_tail = jnp.concatenate([w_in[..., kr], w_in[..., lrf], w_in[..., lrb],
                              jnp.zeros((L, d, TAIL_COLS - MLA_ROPE - 2 * GLA_GATE_RANK), w_in.dtype)],
                             axis=-1).astype(BF16)
    wq = mla_w_q_up.reshape(L, MLA_LORA, MLA_HEADS, MLA_QK)
    wq_rope = jnp.pad(wq[..., MLA_NOPE:], ((0, 0), (0, 0), (0, 0), (0, V7X_LANES - MLA_ROPE)))
    wq_p = jnp.concatenate([wq[..., :MLA_NOPE].reshape(L, MLA_LORA, -1),
                            wq_rope.reshape(L, MLA_LORA, -1)], axis=-1).astype(BF16)
    wkv = mla_w_kv_up.reshape(L, MLA_LORA, MLA_HEADS, MLA_NOPE + MLA_V)
    wkv_p = jnp.concatenate([wkv[..., :MLA_NOPE].reshape(L, MLA_LORA, -1),
                             wkv[..., MLA_NOPE:].reshape(L, MLA_LORA, -1)], axis=-1).astype(BF16)

    def split_gain(g):
        return (g[:, None, :MLA_NOPE],
                jnp.pad(g[:, None, MLA_NOPE:], ((0, 0), (0, 0), (0, V7X_LANES - MLA_ROPE))))

    qgn, qgr = split_gain(mla_q_head_g)
    kgn, kgr = split_gain(mla_k_head_g)
    wg = jnp.zeros((L, 2, TAIL_COLS, GLA_HEADS * GLA_DK), F32)
    wg = wg.at[:, 0, TAIL_LRF:TAIL_LRF + GLA_GATE_RANK].set(gla_w_gate_up[:, 0])
    wg = wg.at[:, 1, TAIL_LRB:TAIL_LRB + GLA_GATE_RANK].set(gla_w_gate_up[:, 1])
    return w_main, w_tail, wq_p, wkv_p, (qgn, qgr, kgn, kgr), wg


def kernel(x, c, ctx, c_ctx, w_mod, b_mod, norm1_g, norm2_g, w_in, mla_q_norm_g, mla_w_q_up,
           mla_kv_norm_g, mla_w_kv_up, mla_q_head_g, mla_k_head_g, gla_w_gate_up, gla_b_gate,
           gla_out_norm_g, pool_w, pool_scale, w_branch, w_out, ffn_w1, ffn_w3, ffn_w2):
    batch, seq, d = x.shape
    n_ctx = ctx.shape[1]
    depth = w_mod.shape[0]
    assert batch + 1 <= 8 and seq % GLA_CHUNK == 0 and n_ctx % GLA_CHUNK == 0

    c8 = jnp.zeros((8, d), F32).at[:batch].set(c).at[batch].set(c_ctx)
    mods_all = _mods_call(c8, w_mod, b_mod).reshape(depth, 8 * N_MOD, 1, d)

    w_main, w_tail, wq_p, wkv_p, (qgn, qgr, kgn, kgr), wg = _prep_weights(
        w_in, mla_w_q_up, mla_w_kv_up, mla_q_head_g, mla_k_head_g, gla_w_gate_up)
    w_branch_b = w_branch.astype(BF16)
    w_out_b = w_out.astype(BF16)
    w1_b, w3_b, w2_b = ffn_w1.astype(BF16), ffn_w3.astype(BF16), ffn_w2.astype(BF16)
    pool_w_b = pool_w.astype(BF16)
    rope_tabs = _rope_tables(seq)
    gla_consts = _gla_constants()
    bands = _pool_bands()
    n_main = w_main.shape[2]

    tok_l = _Tokens(batch * seq, seq, 0)
    tok_c = _Tokens(batch * n_ctx, None, batch)
    xl = x.reshape(batch * seq, d)
    xc = ctx.reshape(batch * n_ctx, d)

    for l in range(depth):
        last = l == depth - 1
        mods = mods_all[l]
        g1 = norm1_g[l][None, :]
        g2 = norm2_g[l][None, :]
        gains = (mla_q_norm_g[l][None, :], mla_kv_norm_g[l][None, :], qgn[l], qgr[l], kgn[l], kgr[l])

        z_l = _norm_matmul_call(tok_l, xl, g1, mods, 0, w_main[l], n_main, 1536, BF16, "in_proj")
        zt_l = _norm_matmul_call(tok_l, xl, g1, mods, 0, w_tail[l], TAIL_COLS, TAIL_COLS, F32, "in_tail")
        z_c = _norm_matmul_call(tok_c, xc, g1, mods, 0, w_main[l], Z_KV_END if last else n_main,
                                1536, BF16, "in_proj_ctx")
        zt_c = _norm_matmul_call(tok_c, xc, g1, mods, 0, w_tail[l], TAIL_COLS, TAIL_COLS, F32, "in_tail_ctx")

        q_l, k_l, v_l = _mla_prep_call(z_l, zt_l, wq_p[l], wkv_p[l], gains, rope_tabs, seq)
        q_c, k_c, v_c = _mla_prep_call(z_c, zt_c, wq_p[l], wkv_p[l], gains, None, n_ctx)
        mla_l = _attn_lat_call(q_l, k_l, v_l, k_c, v_c, batch, seq, n_ctx)

        gla_l, gla_c = _gla_call(z_l, zt_l, z_c, zt_c, wg[l], gla_b_gate[l][:, None, :],
                                 gla_out_norm_g[l][None, :], gla_consts, batch, seq, n_ctx, not last)
        pool_l, pool_c = _pool_call(z_l, None if last else z_c, bands, pool_w_b[l],
                                    pool_scale[l][None, :], batch, seq, n_ctx)

        def finish_layer(tok, xs, z, pool_o, mla_o, gla_o):
            merged = _merge_call(tok, pool_o, mla_o, gla_o, z, w_branch_b[l], d)
            xs = _matmul_residual_call(tok, merged, w_out_b[l], xs, mods, 2, 1024, "out_proj")
            up = _swiglu_up_call(tok, xs, g2, mods, w1_b[l], w3_b[l])
            return _matmul_residual_call(tok, up, w2_b[l], xs, mods, 5, 512, "swiglu_down")

        xl = finish_layer(tok_l, xl, z_l, pool_l, mla_l, gla_l)
        if not last:
            mla_c = _attn_ctx_call(q_c, k_c, v_c, batch, n_ctx)
            xc = finish_layer(tok_c, xc, z_c, pool_c, mla_c, gla_c)
    return xl.reshape(batch, seq, d)
```

```python
import functools

import numpy as np
import jax
import jax.numpy as jnp
from jax import lax
from jax.experimental import pallas as pl
from jax.experimental.pallas import tpu as pltpu

F32 = jnp.float32
BF16 = jnp.bfloat16

GRID_W = 64
RMS_EPS = 1e-6
N_BRANCH = 3
BRANCH_DIM = 1024
POOL_WINDOWS = (2, 4, 8, 16)
N_POOL_GROUPS = len(POOL_WINDOWS)
POOL_GROUP = BRANCH_DIM // N_POOL_GROUPS
MLA_HEADS = 8
MLA_LORA = 512
MLA_NOPE = 128
MLA_ROPE = 64
MLA_V = BRANCH_DIM // MLA_HEADS
MLA_QK = MLA_NOPE + MLA_ROPE
ROPE_BASE = 10000.0
GLA_HEADS = 4
GLA_DK = 256
GLA_DV = BRANCH_DIM // GLA_HEADS
GLA_GATE_RANK = 16
GLA_GATE_TAU = 16.0
N_MOD = 6

V7X_VMEM_BYTES = 64 * 1024 * 1024
V7X_LANES = 128
MXU_DIM = 256

MLA_HEAD_PAD = 2 * V7X_LANES
GLA_BLOCK = 256
GLA_SPANS = (128, 64, 32, 16)
GLA_SUB = 16
POOL_HALO = 16
TAIL_COLS = V7X_LANES
TAIL_LRF = MLA_ROPE
TAIL_LRB = MLA_ROPE + GLA_GATE_RANK

Z_CKV = 0
Z_CQ = Z_CKV + MLA_LORA
Z_GK = Z_CQ + MLA_LORA
Z_GV = Z_GK + GLA_HEADS * GLA_DK
Z_GQ = Z_GV + GLA_HEADS * GLA_DV
Z_GG = Z_GQ + GLA_HEADS * GLA_DK
Z_PIN = Z_GG + GLA_HEADS * GLA_DV
Z_GATE = Z_PIN + BRANCH_DIM
Z_KV_END = Z_GQ


def _divisor_tile(n, pref, align):
    if n <= pref:
        return n
    best = None
    for t in range(align, pref + 1, align):
        if n % t == 0:
            best = t
    assert best is not None, (n, pref, align)
    return best


def _params(semantics, vmem_bytes):
    limit = int(min(V7X_VMEM_BYTES - 6 * 1024 * 1024, max(vmem_bytes, 16 * 1024 * 1024)))
    return pltpu.CompilerParams(dimension_semantics=semantics, vmem_limit_bytes=limit)


def _nbytes(shape, dtype):
    return int(np.prod(shape)) * jnp.dtype(dtype).itemsize


def _sigmoid(x):
    return 1.0 / (1.0 + jnp.exp(-x))


def _k_mods(c_ref, w_ref, b_ref, o_ref):
    cv = c_ref[...]
    a = cv * _sigmoid(cv)
    o_ref[...] = jnp.dot(a, w_ref[...], precision=lax.Precision.HIGHEST,
                         preferred_element_type=F32) + b_ref[...]


def _mods_call(c8, w_mod, b_mod):
    L, D, N = w_mod.shape
    tn = _divisor_tile(N, 2048, V7X_LANES)
    vm = 2 * _nbytes((D, tn), F32) + 4 * _nbytes((8, tn), F32) + 2 * _nbytes((8, D), F32) + (4 << 20)
    return pl.pallas_call(
        _k_mods,
        grid=(L, N // tn),
        in_specs=[pl.BlockSpec((8, D), lambda l, j: (0, 0)),
                  pl.BlockSpec((None, D, tn), lambda l, j: (l, 0, j)),
                  pl.BlockSpec((None, 1, tn), lambda l, j: (l, 0, j))],
        out_specs=pl.BlockSpec((None, 8, tn), lambda l, j: (l, 0, j)),
        out_shape=jax.ShapeDtypeStruct((L, 8, N), F32),
        compiler_params=_params(("arbitrary", "arbitrary"), vm),
        name="mods",
    )(c8, w_mod, b_mod.reshape(L, 1, N))


NORM_ROWS = 256


def _norm_mod_rows(x_ref, g_ref, sh_ref, sc_ref, h_ref):
    tm = x_ref.shape[0]
    step = min(NORM_ROWS, tm)
    g = g_ref[...]
    sh = sh_ref[...]
    sc1 = 1.0 + sc_ref[...]

    def body(r, carry):
        rows = pl.ds(pl.multiple_of(r * step, step), step)
        x = x_ref[rows, :]
        ms = jnp.mean(x * x, axis=-1, keepdims=True)
        y = x * lax.rsqrt(ms + RMS_EPS) * g
        h_ref[rows, :] = (y * sc1 + sh).astype(h_ref.dtype)
        return carry

    lax.fori_loop(0, tm // step, body, 0)


def _k_norm_matmul(x_ref, g_ref, sh_ref, sc_ref, w_ref, o_ref, h_scr):
    @pl.when(pl.program_id(1) == 0)
    def _():
        _norm_mod_rows(x_ref, g_ref, sh_ref, sc_ref, h_scr)

    o_ref[...] = jnp.dot(h_scr[...], w_ref[...], preferred_element_type=F32).astype(o_ref.dtype)


def _k_norm_swiglu_up(x_ref, g_ref, sh_ref, sc_ref, w1_ref, w3_ref, o_ref, h_scr):
    @pl.when(pl.program_id(1) == 0)
    def _():
        _norm_mod_rows(x_ref, g_ref, sh_ref, sc_ref, h_scr)

    h = h_scr[...]
    a = jnp.dot(h, w1_ref[...], preferred_element_type=F32)
    b = jnp.dot(h, w3_ref[...], preferred_element_type=F32)
    o_ref[...] = (a * _sigmoid(a) * b).astype(o_ref.dtype)


def _k_matmul_residual(a_ref, w_ref, x_ref, gate_ref, o_ref):
    y = jnp.dot(a_ref[...], w_ref[...], preferred_element_type=F32)
    o_ref[...] = x_ref[...] + gate_ref[...] * y


def _k_merge(p_ref, m_ref, gl_ref, g0_ref, g1_ref, g2_ref, wb_ref, o_ref):
    acc = None
    for n, (br, gt) in enumerate(((p_ref, g0_ref), (m_ref, g1_ref), (gl_ref, g2_ref))):
        proj = jnp.dot(br[...], wb_ref[n], preferred_element_type=F32)
        term = _sigmoid(gt[...].astype(F32)) * proj
        acc = term if acc is None else acc + term
    o_ref[...] = acc.astype(o_ref.dtype)


class _Tokens:
    def __init__(self, n_rows, rows_per_mod, mod_row_base, tm_pref=1024):
        self.m = n_rows
        if rows_per_mod is None:
            self.tm = _divisor_tile(n_rows, tm_pref, 8)
            self.mod_row = lambda i: mod_row_base
        else:
            self.tm = _divisor_tile(rows_per_mod, tm_pref, 8)
            per = rows_per_mod // self.tm
            self.mod_row = lambda i: mod_row_base + i // per
        self.n_tiles = n_rows // self.tm

    def mod_spec(self, k, d, tn=None):
        if tn is None:
            return pl.BlockSpec((None, 1, d), lambda i, j: (self.mod_row(i) * N_MOD + k, 0, 0))
        return pl.BlockSpec((None, 1, tn), lambda i, j: (self.mod_row(i) * N_MOD + k, 0, j))


def _layer_spec(layer, block, col_axis_index):
    zeros = (0,) * (len(block) - 1)
    return pl.BlockSpec((None,) + block, lambda i, j: (layer,) + zeros + (col_axis_index(j),))


def _norm_matmul_call(tok, x, g, mods, k_shift, w, layer, n_cols, tn_pref, out_dtype, name):
    m, d = x.shape
    tm = tok.tm
    tn = _divisor_tile(n_cols, tn_pref, V7X_LANES)
    vm = (2 * _nbytes((tm, d), F32) + _nbytes((tm, d), BF16) + 2 * _nbytes((d, tn), BF16)
          + 2 * _nbytes((tm, tn), out_dtype) + 2 * _nbytes((tm, tn), F32) + (4 << 20))
    return pl.pallas_call(
        _k_norm_matmul,
        grid=(tok.n_tiles, n_cols // tn),
        in_specs=[pl.BlockSpec((tm, d), lambda i, j: (i, 0)),
                  pl.BlockSpec((1, d), lambda i, j: (0, 0)),
                  tok.mod_spec(k_shift, d),
                  tok.mod_spec(k_shift + 1, d),
                  _layer_spec(layer, (d, tn), lambda j: j)],
        out_specs=pl.BlockSpec((tm, tn), lambda i, j: (i, j)),
        out_shape=jax.ShapeDtypeStruct((m, n_cols), out_dtype),
        scratch_shapes=[pltpu.VMEM((tm, d), BF16)],
        compiler_params=_params(("arbitrary", "arbitrary"), vm),
        name=name,
    )(x, g, mods, mods, w)


def _swiglu_up_call(tok, x, g, mods, w1, w3, layer):
    m, d = x.shape
    f = w1.shape[2]
    tm = tok.tm
    tn = _divisor_tile(f, 512, V7X_LANES)
    vm = (2 * _nbytes((tm, d), F32) + _nbytes((tm, d), BF16) + 4 * _nbytes((d, tn), BF16)
          + 2 * _nbytes((tm, tn), BF16) + 4 * _nbytes((tm, tn), F32) + (4 << 20))
    return pl.pallas_call(
        _k_norm_swiglu_up,
        grid=(tok.n_tiles, f // tn),
        in_specs=[pl.BlockSpec((tm, d), lambda i, j: (i, 0)),
                  pl.BlockSpec((1, d), lambda i, j: (0, 0)),
                  tok.mod_spec(3, d),
                  tok.mod_spec(4, d),
                  _layer_spec(layer, (d, tn), lambda j: j),
                  _layer_spec(layer, (d, tn), lambda j: j)],
        out_specs=pl.BlockSpec((tm, tn), lambda i, j: (i, j)),
        out_shape=jax.ShapeDtypeStruct((m, f), BF16),
        scratch_shapes=[pltpu.VMEM((tm, d), BF16)],
        compiler_params=_params(("arbitrary", "arbitrary"), vm),
        name="swiglu_up",
    )(x, g, mods, mods, w1, w3)


def _matmul_residual_call(tok, a, w, layer, x, mods, k_gate, tn_pref, name):
    m, kdim = a.shape
    d = w.shape[2]
    tm = tok.tm
    tn = _divisor_tile(d, tn_pref, V7X_LANES)
    vm = (2 * _nbytes((tm, kdim), BF16) + 2 * _nbytes((kdim, tn), BF16)
          + 5 * _nbytes((tm, tn), F32) + (4 << 20))
    return pl.pallas_call(
        _k_matmul_residual,
        grid=(tok.n_tiles, d // tn),
        in_specs=[pl.BlockSpec((tm, kdim), lambda i, j: (i, 0)),
                  _layer_spec(layer, (kdim, tn), lambda j: j),
                  pl.BlockSpec((tm, tn), lambda i, j: (i, j)),
                  tok.mod_spec(k_gate, d, tn)],
        out_specs=pl.BlockSpec((tm, tn), lambda i, j: (i, j)),
        out_shape=jax.ShapeDtypeStruct((m, d), F32),
        compiler_params=_params(("arbitrary", "arbitrary"), vm),
        name=name,
    )(a, w, x, mods)


def _merge_call(tok, pool_o, mla_o, gla_o, z, w_branch, layer, d):
    m = pool_o.shape[0]
    tm = tok.tm
    tn = _divisor_tile(d, 512, V7X_LANES)
    assert Z_GATE % tn == 0
    g0 = Z_GATE // tn
    per = d // tn
    br_spec = pl.BlockSpec((tm, BRANCH_DIM), lambda i, j: (i, 0))
    vm = (6 * _nbytes((tm, BRANCH_DIM), BF16) + 6 * _nbytes((tm, tn), BF16)
          + 2 * _nbytes((N_BRANCH, BRANCH_DIM, tn), BF16) + 2 * _nbytes((tm, tn), BF16)
          + 4 * _nbytes((tm, tn), F32) + (4 << 20))
    return pl.pallas_call(
        _k_merge,
        grid=(tok.n_tiles, per),
        in_specs=[br_spec, br_spec, br_spec,
                  pl.BlockSpec((tm, tn), lambda i, j: (i, g0 + j)),
                  pl.BlockSpec((tm, tn), lambda i, j: (i, g0 + per + j)),
                  pl.BlockSpec((tm, tn), lambda i, j: (i, g0 + 2 * per + j)),
                  _layer_spec(layer, (N_BRANCH, BRANCH_DIM, tn), lambda j: j)],
        out_specs=pl.BlockSpec((tm, tn), lambda i, j: (i, j)),
        out_shape=jax.ShapeDtypeStruct((m, d), BF16),
        compiler_params=_params(("arbitrary", "arbitrary"), vm),
        name="merge",
    )(pool_o, mla_o, gla_o, z, z, z, w_branch)


def _k_mla_prep(use_rope, zkv_ref, zq_ref, zt_ref, wq_ref, wkv_ref, qng_ref, kvng_ref,
                qgn_ref, qgr_ref, kgn_ref, kgr_ref, *rest):
    if use_rope:
        c_ref, sa_ref, sb_ref, q_out, k_out, v_out = rest
        rc, rsa, rsb = c_ref[...], sa_ref[...], sb_ref[...]
    else:
        q_out, k_out, v_out = rest
    hn = MLA_HEADS * MLA_NOPE

    def rms(x, g):
        xf = x.astype(F32)
        return xf * lax.rsqrt(jnp.mean(xf * xf, axis=-1, keepdims=True) + RMS_EPS) * g

    def rope(x):
        if not use_rope:
            return x
        return (x * rc + pltpu.roll(x, V7X_LANES - MLA_ROPE // 4, 1) * rsa
                + pltpu.roll(x, MLA_ROPE // 4, 1) * rsb)

    q = jnp.dot(rms(zq_ref[...], qng_ref[...]).astype(BF16), wq_ref[...], preferred_element_type=F32)
    kv = jnp.dot(rms(zkv_ref[...], kvng_ref[...]).astype(BF16), wkv_ref[...], preferred_element_type=F32)
    v_out[...] = kv[:, hn:].astype(v_out.dtype)

    zt = zt_ref[...]
    lane = lax.broadcasted_iota(jnp.int32, zt.shape, 1)
    kr = jnp.where(lane < MLA_ROPE, zt, 0.0)
    kr_ss = jnp.sum(kr * kr, axis=-1, keepdims=True)
    qgn, qgr, kgn, kgr = qgn_ref[...], qgr_ref[...], kgn_ref[...], kgr_ref[...]
    inv_dim = 1.0 / MLA_QK
    q_scale = MLA_QK ** -0.5
    for h in range(MLA_HEADS):
        lo = h * MLA_HEAD_PAD
        qn = q[:, h * MLA_NOPE:(h + 1) * MLA_NOPE]
        qr = q[:, hn + h * V7X_LANES: hn + (h + 1) * V7X_LANES]
        ss = jnp.sum(qn * qn, axis=-1, keepdims=True) + jnp.sum(qr * qr, axis=-1, keepdims=True)
        r = lax.rsqrt(ss * inv_dim + RMS_EPS) * q_scale
        q_out[:, lo:lo + MLA_NOPE] = (qn * r * qgn).astype(q_out.dtype)
        q_out[:, lo + MLA_NOPE:lo + MLA_HEAD_PAD] = rope(qr * r * qgr).astype(q_out.dtype)
        kn = kv[:, h * MLA_NOPE:(h + 1) * MLA_NOPE]
        rk = lax.rsqrt((jnp.sum(kn * kn, axis=-1, keepdims=True) + kr_ss) * inv_dim + RMS_EPS)
        k_out[:, lo:lo + MLA_NOPE] = (kn * rk * kgn).astype(k_out.dtype)
        k_out[:, lo + MLA_NOPE:lo + MLA_HEAD_PAD] = rope(kr * rk * kgr).astype(k_out.dtype)


def _mla_prep_call(z, zt, wq, wkv, layer, gains, rope_tabs, seq_len):
    m = z.shape[0]
    tm = _divisor_tile(seq_len, 256, 16)
    use_rope = rope_tabs is not None
    hq = MLA_HEADS * MLA_HEAD_PAD
    hv = MLA_HEADS * MLA_V
    full = lambda shape: pl.BlockSpec(shape, lambda i: tuple(0 for _ in shape))
    in_specs = [pl.BlockSpec((tm, MLA_LORA), lambda i: (i, Z_CKV // MLA_LORA)),
                pl.BlockSpec((tm, MLA_LORA), lambda i: (i, Z_CQ // MLA_LORA)),
                pl.BlockSpec((tm, TAIL_COLS), lambda i: (i, 0)),
                pl.BlockSpec((None,) + wq.shape[1:], lambda i: (layer, 0, 0)),
                pl.BlockSpec((None,) + wkv.shape[1:], lambda i: (layer, 0, 0))]
    in_specs += [full(g.shape) for g in gains]
    args = [z, z, zt, wq, wkv, *gains]
    if use_rope:
        per = seq_len // tm
        in_specs += [pl.BlockSpec((tm, V7X_LANES), lambda i: (i % per, 0))] * 3
        args += list(rope_tabs)
    vm = (8 * _nbytes((tm, hq), F32) + 4 * _nbytes(wq.shape[1:], BF16) + (8 << 20))
    return pl.pallas_call(
        functools.partial(_k_mla_prep, use_rope),
        grid=(m // tm,),
        in_specs=in_specs,
        out_specs=[pl.BlockSpec((tm, hq), lambda i: (i, 0)),
                   pl.BlockSpec((tm, hq), lambda i: (i, 0)),
                   pl.BlockSpec((tm, hv), lambda i: (i, 0))],
        out_shape=[jax.ShapeDtypeStruct((m, hq), BF16),
                   jax.ShapeDtypeStruct((m, hq), BF16),
                   jax.ShapeDtypeStruct((m, hv), BF16)],
        compiler_params=_params(("arbitrary",), vm),
        name="mla_prep_rope" if use_rope else "mla_prep",
    )(*args)


_NT = (((1,), (1,)), ((), ()))
_TN = (((0,), (0,)), ((), ()))


ATTN_HEADS_PER_STEP = 2


def _k_attn_lat(q_ref, kl_ref, kc_ref, vl_ref, vc_ref, o_ref):
    for h in range(ATTN_HEADS_PER_STEP):
        qk = slice(h * MLA_HEAD_PAD, (h + 1) * MLA_HEAD_PAD)
        vv = slice(h * MLA_V, (h + 1) * MLA_V)
        q = q_ref[:, qk]
        sl = lax.dot_general(q, kl_ref[:, qk], _NT, preferred_element_type=F32)
        sc = lax.dot_general(q, kc_ref[:, qk], _NT, preferred_element_type=F32)
        mx = jnp.maximum(jnp.max(sl, axis=-1, keepdims=True), jnp.max(sc, axis=-1, keepdims=True))
        p_l = jnp.exp(sl - mx)
        p_c = jnp.exp(sc - mx)
        den = jnp.sum(p_l, axis=-1, keepdims=True) + jnp.sum(p_c, axis=-1, keepdims=True)
        o = (jnp.dot(p_c.astype(BF16), vc_ref[:, vv], preferred_element_type=F32)
             + jnp.dot(p_l.astype(BF16), vl_ref[:, vv], preferred_element_type=F32))
        o_ref[:, vv] = (o / den).astype(o_ref.dtype)


def _k_attn_ctx(q_ref, kc_ref, vc_ref, o_ref):
    for h in range(ATTN_HEADS_PER_STEP):
        qk = slice(h * MLA_HEAD_PAD, (h + 1) * MLA_HEAD_PAD)
        vv = slice(h * MLA_V, (h + 1) * MLA_V)
        sc = lax.dot_general(q_ref[:, qk], kc_ref[:, qk], _NT, preferred_element_type=F32)
        p_c = jnp.exp(sc - jnp.max(sc, axis=-1, keepdims=True))
        den = jnp.sum(p_c, axis=-1, keepdims=True)
        o = jnp.dot(p_c.astype(BF16), vc_ref[:, vv], preferred_element_type=F32)
        o_ref[:, vv] = (o / den).astype(o_ref.dtype)


def _attn_lat_call(q_l, k_l, v_l, k_c, v_c, batch, seq, ctx):
    tq = _divisor_tile(seq, 256, 16)
    nq = seq // tq
    hp = ATTN_HEADS_PER_STEP
    wqk, wv = hp * MLA_HEAD_PAD, hp * MLA_V
    vm = (4 * hp * _nbytes((tq, seq + ctx), F32) + 2 * _nbytes((seq + ctx, wqk + wv), BF16) + (8 << 20))
    return pl.pallas_call(
        _k_attn_lat,
        grid=(batch, MLA_HEADS // hp, nq),
        in_specs=[pl.BlockSpec((tq, wqk), lambda b, h, i: (b * nq + i, h)),
                  pl.BlockSpec((seq, wqk), lambda b, h, i: (b, h)),
                  pl.BlockSpec((ctx, wqk), lambda b, h, i: (b, h)),
                  pl.BlockSpec((seq, wv), lambda b, h, i: (b, h)),
                  pl.BlockSpec((ctx, wv), lambda b, h, i: (b, h))],
        out_specs=pl.BlockSpec((tq, wv), lambda b, h, i: (b * nq + i, h)),
        out_shape=jax.ShapeDtypeStruct((batch * seq, MLA_HEADS * MLA_V), BF16),
        compiler_params=_params(("arbitrary", "arbitrary", "arbitrary"), vm),
        name="attn_lat",
    )(q_l, k_l, k_c, v_l, v_c)


def _attn_ctx_call(q_c, k_c, v_c, batch, ctx):
    hp = ATTN_HEADS_PER_STEP
    wqk, wv = hp * MLA_HEAD_PAD, hp * MLA_V
    vm = 4 * hp * _nbytes((ctx, ctx), F32) + (8 << 20)
    return pl.pallas_call(
        _k_attn_ctx,
        grid=(batch, MLA_HEADS // hp),
        in_specs=[pl.BlockSpec((ctx, wqk), lambda b, h: (b, h)),
                  pl.BlockSpec((ctx, wqk), lambda b, h: (b, h)),
                  pl.BlockSpec((ctx, wv), lambda b, h: (b, h))],
        out_specs=pl.BlockSpec((ctx, wv), lambda b, h: (b, h)),
        out_shape=jax.ShapeDtypeStruct((batch * ctx, MLA_HEADS * MLA_V), BF16),
        compiler_params=_params(("arbitrary", "arbitrary"), vm),
        name="attn_ctx",
    )(q_c, k_c, v_c)


def _gla_constants():
    c, sub = GLA_BLOCK, GLA_SUB
    t = np.arange(c)[:, None]
    s = np.arange(c)[None, :]
    tris, masks = [], []
    for d in range(2):
        step = -1 if d == 0 else 1
        tris.append(s <= t if d == 0 else s >= t)
        lv = [(((t // sp) % 2) == (1 if d == 0 else 0)) & (s // sp == t // sp + step)
              for sp in GLA_SPANS]
        lv.append((t // sub == s // sub) & (s <= t if d == 0 else s >= t))
        assert (np.sum(lv, axis=0) == tris[-1]).all()
        masks.append(np.stack(lv, 0))
    return (jnp.asarray(np.stack(tris, 0), BF16), jnp.asarray(np.stack(masks, 0), F32))


def _gla_ref_rows(cum, d, span):
    c = cum.shape[0]
    pieces = []
    if span is None:
        for g in range(c // GLA_SUB):
            idx = g * GLA_SUB - 1 if d == 0 else (g + 1) * GLA_SUB
            row = cum[idx:idx + 1] if 0 <= idx < c else jnp.zeros_like(cum[0:1])
            pieces.append(jnp.broadcast_to(row, (GLA_SUB, cum.shape[1])))
    else:
        for g in range(c // (2 * span)):
            idx = g * 2 * span + (span - 1 if d == 0 else span)
            pieces.append(jnp.broadcast_to(cum[idx:idx + 1], (2 * span, cum.shape[1])))
    return pieces[0] if len(pieces) == 1 else jnp.concatenate(pieces, axis=0)


def _k_gla(need_ctx_out, ql_ref, kl_ref, vl_ref, ggl_ref, ztl_ref, kc_ref, vc_ref, ztc_ref,
           wg_ref, bg_ref, gn_ref, tri_ref, masks_ref, *rest):
    if need_ctx_out:
        qc_ref, ggc_ref, ol_ref, oc_ref, acc_l, acc_c, st_ref = rest
    else:
        ol_ref, acc_l, st_ref = rest
        qc_ref = ggc_ref = oc_ref = acc_c = None
    c = GLA_BLOCK
    q_scale = GLA_DK ** -0.5
    inv_tau = 1.0 / GLA_GATE_TAU

    def chunk(d, q_ref, k_ref, v_ref, zt_ref, j, acc_ref):
        rows = pl.ds(pl.multiple_of(j * c, c), c)
        k = k_ref[rows, :].astype(F32)
        v = v_ref[rows, :]
        zg = jnp.dot(zt_ref[rows, :].astype(BF16), wg_ref[d], preferred_element_type=F32) + bg_ref[d]
        la = (jnp.minimum(zg, 0.0) - jnp.log1p(jnp.exp(-jnp.abs(zg)))) * inv_tau
        hi = la.astype(BF16)
        lo = (la - hi.astype(F32)).astype(BF16)
        tri = tri_ref[d]
        cum = (jnp.dot(tri, hi, preferred_element_type=F32) + jnp.dot(tri, lo, preferred_element_type=F32))
        tot = cum[c - 1:c] if d == 0 else cum[0:1]
        st = st_ref[d]
        if acc_ref is not None:
            q = q_ref[rows, :].astype(F32) * q_scale
            inter = lax.dot_general((q * jnp.exp(cum)).astype(BF16), st.astype(BF16), _NT,
                                    preferred_element_type=F32)
            a = None
            for lvl, span in enumerate(GLA_SPANS + (None,)):
                dlt = cum - _gla_ref_rows(cum, d, span)
                if span is None:
                    qe = (q * jnp.exp(dlt)).astype(BF16)
                    ke = (k * jnp.exp(-dlt)).astype(BF16)
                else:
                    e = jnp.exp(-jnp.abs(dlt))
                    qe = (q * e).astype(BF16)
                    ke = (k * e).astype(BF16)
                part = masks_ref[d, lvl] * lax.dot_general(qe, ke, _NT, preferred_element_type=F32)
                a = part if a is None else a + part
            intra = jnp.dot(a.astype(BF16), v, preferred_element_type=F32)
            acc_ref[rows, :] += inter + intra
        kd = (k * jnp.exp(tot - cum)).astype(BF16)
        st_ref[d] = st * jnp.exp(tot) + lax.dot_general(v, kd, _TN, preferred_element_type=F32)

    def scan(q_ref, k_ref, v_ref, zt_ref, acc_ref):
        n = k_ref.shape[0] // c
        if acc_ref is not None:
            acc_ref[...] = jnp.zeros(acc_ref.shape, acc_ref.dtype)

        def body(i, carry):
            chunk(0, q_ref, k_ref, v_ref, zt_ref, i, acc_ref)
            chunk(1, q_ref, k_ref, v_ref, zt_ref, n - 1 - i, acc_ref)
            return carry

        lax.fori_loop(0, n, body, 0)

    def finish(acc_ref, gg_ref, o_ref):
        n_rows = acc_ref.shape[0]
        step = min(NORM_ROWS, n_rows)
        gn = gn_ref[...]

        def body(r, carry):
            rows = pl.ds(pl.multiple_of(r * step, step), step)
            o = acc_ref[rows, :]
            y = o * lax.rsqrt(jnp.mean(o * o, axis=-1, keepdims=True) + RMS_EPS) * gn
            g = gg_ref[rows, :].astype(F32)
            o_ref[rows, :] = (y * (g * _sigmoid(g))).astype(o_ref.dtype)
            return carry

        lax.fori_loop(0, n_rows // step, body, 0)

    st_ref[...] = jnp.zeros(st_ref.shape, st_ref.dtype)
    scan(qc_ref, kc_ref, vc_ref, ztc_ref, acc_c)
    scan(ql_ref, kl_ref, vl_ref, ztl_ref, acc_l)
    finish(acc_l, ggl_ref, ol_ref)
    if need_ctx_out:
        finish(acc_c, ggc_ref, oc_ref)


def _gla_call(z_l, zt_l, z_c, zt_c, wg, bg, gn, consts, batch, seq, ctx, need_ctx_out):
    tri, masks = consts
    blk = lambda rows, col0: pl.BlockSpec((rows, GLA_DK), lambda b, h: (b, col0 // GLA_DK + h))
    tail = lambda rows: pl.BlockSpec((rows, TAIL_COLS), lambda b, h: (b, 0))
    full = lambda a: pl.BlockSpec(a.shape, lambda b, h: tuple(0 for _ in a.shape))
    in_specs = [blk(seq, Z_GQ), blk(seq, Z_GK), blk(seq, Z_GV), blk(seq, Z_GG), tail(seq),
                blk(ctx, Z_GK), blk(ctx, Z_GV), tail(ctx),
                pl.BlockSpec((2, TAIL_COLS, GLA_DK), lambda b, h: (0, 0, h)),
                pl.BlockSpec((2, 1, GLA_DK), lambda b, h: (0, 0, h)),
                full(gn), full(tri), full(masks)]
    args = [z_l, z_l, z_l, z_l, zt_l, z_c, z_c, zt_c, wg, bg, gn, tri, masks]
    out_specs = [pl.BlockSpec((seq, GLA_DV), lambda b, h: (b, h))]
    out_shape = [jax.ShapeDtypeStruct((batch * seq, GLA_HEADS * GLA_DV), BF16)]
    scratch = [pltpu.VMEM((seq, GLA_DV), F32)]
    if need_ctx_out:
        in_specs += [blk(ctx, Z_GQ), blk(ctx, Z_GG)]
        args += [z_c, z_c]
        out_specs.append(pl.BlockSpec((ctx, GLA_DV), lambda b, h: (b, h)))
        out_shape.append(jax.ShapeDtypeStruct((batch * ctx, GLA_HEADS * GLA_DV), BF16))
        scratch.append(pltpu.VMEM((ctx, GLA_DV), F32))
    scratch.append(pltpu.VMEM((2, GLA_DV, GLA_DK), F32))
    vm = (10 * _nbytes((seq + ctx, GLA_DK), BF16) + 2 * _nbytes((seq + ctx, TAIL_COLS), F32)
          + _nbytes((seq + ctx, GLA_DV), F32) + (12 << 20))
    outs = pl.pallas_call(
        functools.partial(_k_gla, need_ctx_out),
        grid=(batch, GLA_HEADS),
        in_specs=in_specs,
        out_specs=out_specs,
        out_shape=out_shape,
        scratch_shapes=scratch,
        compiler_params=_params(("arbitrary", "arbitrary"), vm),
        name="gla" if need_ctx_out else "gla_last",
    )(*args)
    return (outs[0], outs[1]) if need_ctx_out else (outs[0], None)


POOL_ROWS = 256


def _pool_bands():
    t = np.arange(POOL_ROWS)[:, None]
    m = np.arange(POOL_ROWS + 2 * POOL_HALO)[None, :] - POOL_HALO
    bands = [(m >= t - w // 2) & (m <= t + w // 2 - 1) for w in POOL_WINDOWS]
    return jnp.asarray(np.stack(bands, 0), BF16)


def _k_pool(ul_ref, uc_ref, band_ref, w_ref, sc_ref, ol_ref, oc_ref, padl_ref, padc_ref):
    half = lax.shift_left(jnp.int32(1), pl.program_id(1))
    w = w_ref[...]
    scale = sc_ref[...]

    def run(u_ref, pad_ref, o_ref):
        n = u_ref.shape[0]
        rows = min(POOL_ROWS, n)
        zeros = jnp.zeros((POOL_HALO, pad_ref.shape[1]), pad_ref.dtype)
        pad_ref[0:POOL_HALO, :] = zeros
        pad_ref[POOL_HALO + n:2 * POOL_HALO + n, :] = zeros
        pad_ref[POOL_HALO:POOL_HALO + n, :] = u_ref[...]
        band = band_ref[0:rows, 0:rows + 2 * POOL_HALO]

        def body(r, carry):
            r0 = pl.multiple_of(r * rows, rows)
            win = jnp.dot(band, pad_ref[pl.ds(r0, rows + 2 * POOL_HALO), :],
                          preferred_element_type=F32)
            pos = r0 + lax.broadcasted_iota(jnp.int32, (rows, 1), 0)
            cnt = (jnp.minimum(pos, half) + jnp.minimum(n - pos, half)).astype(F32)
            pooled = win / cnt - u_ref[pl.ds(r0, rows), :].astype(F32)
            y = jnp.dot(pooled.astype(BF16), w, preferred_element_type=F32) * scale
            o_ref[pl.ds(r0, rows), :] = y.astype(o_ref.dtype)
            return carry

        lax.fori_loop(0, n // rows, body, 0)

    run(ul_ref, padl_ref, ol_ref)
    if oc_ref is not None:
        run(uc_ref, padc_ref, oc_ref)


def _k_pool_lat_only(ul_ref, band_ref, w_ref, sc_ref, ol_ref, padl_ref):
    _k_pool(ul_ref, None, band_ref, w_ref, sc_ref, ol_ref, None, padl_ref, None)


def _pool_call(z_l, z_c, bands, pool_w, pool_scale, batch, seq, ctx):
    g0 = Z_PIN // POOL_GROUP
    blk = lambda rows: pl.BlockSpec((rows, POOL_GROUP), lambda b, g: (b, g0 + g))
    oblk = lambda rows: pl.BlockSpec((rows, POOL_GROUP), lambda b, g: (b, g))
    common = [pl.BlockSpec((None,) + bands.shape[1:], lambda b, g: (g, 0, 0)),
              pl.BlockSpec((None, POOL_GROUP, POOL_GROUP), lambda b, g: (g, 0, 0)),
              pl.BlockSpec((1, POOL_GROUP), lambda b, g: (0, g))]
    vm = 8 * _nbytes((seq + ctx, POOL_GROUP), F32) + (8 << 20)
    pad = lambda rows: pltpu.VMEM((rows + 2 * POOL_HALO, POOL_GROUP), BF16)
    if z_c is None:
        out = pl.pallas_call(
            _k_pool_lat_only, grid=(batch, N_POOL_GROUPS),
            in_specs=[blk(seq)] + common, out_specs=oblk(seq),
            out_shape=jax.ShapeDtypeStruct((batch * seq, BRANCH_DIM), BF16),
            scratch_shapes=[pad(seq)],
            compiler_params=_params(("arbitrary", "arbitrary"), vm), name="pool_last",
        )(z_l, bands, pool_w, pool_scale)
        return out, None
    return pl.pallas_call(
        _k_pool, grid=(batch, N_POOL_GROUPS),
        in_specs=[blk(seq), blk(ctx)] + common, out_specs=[oblk(seq), oblk(ctx)],
        out_shape=[jax.ShapeDtypeStruct((batch * seq, BRANCH_DIM), BF16),
                   jax.ShapeDtypeStruct((batch * ctx, BRANCH_DIM), BF16)],
        scratch_shapes=[pad(seq), pad(ctx)],
        compiler_params=_params(("arbitrary", "arbitrary"), vm), name="pool",
    )(z_l, z_c, bands, pool_w, pool_scale)


def _rope_tables(seq):
    half = MLA_ROPE // 2
    quarter = half // 2
    inv_freq = 1.0 / (ROPE_BASE ** (jnp.arange(0, half, 2, dtype=F32) / half))
    t = jnp.arange(seq)
    ang_r = (t // GRID_W).astype(F32)[:, None] * inv_freq[None, :]
    ang_c = (t % GRID_W).astype(F32)[:, None] * inv_freq[None, :]
    cr, sr, cc, sn = jnp.cos(ang_r), jnp.sin(ang_r), jnp.cos(ang_c), jnp.sin(ang_c)
    z = jnp.zeros((seq, quarter), F32)
    zpad = jnp.zeros((seq, V7X_LANES - MLA_ROPE), F32)
    cos = jnp.concatenate([cr, cr, cc, cc, zpad], axis=1)
    sa = jnp.concatenate([-sr, z, -sn, z, zpad], axis=1)
    sb = jnp.concatenate([z, sr, z, sn, zpad], axis=1)
    return cos, sa, sb


RELAYOUT_ROWS = 128


def _k_relayout(moves, w_ref, o_ref):
    for src, dst, width in moves:
        o_ref[:, dst:dst + width] = w_ref[:, src:src + width].astype(o_ref.dtype)


def _relayout_call(w_in, groups):
    L, d, n_in = w_in.shape
    moves, dst = [], 0
    for s in groups:
        moves.append((s.start, dst, s.stop - s.start))
        dst += s.stop - s.start
    tr = _divisor_tile(d, RELAYOUT_ROWS, 16)
    vm = 2 * _nbytes((tr, n_in), F32) + 2 * _nbytes((tr, dst), BF16) + (8 << 20)
    return pl.pallas_call(
        functools.partial(_k_relayout, tuple(moves)),
        grid=(L, d // tr),
        in_specs=[pl.BlockSpec((None, tr, n_in), lambda l, r: (l, r, 0))],
        out_specs=pl.BlockSpec((None, tr, dst), lambda l, r: (l, r, 0)),
        out_shape=jax.ShapeDtypeStruct((L, d, dst), BF16),
        compiler_params=_params(("arbitrary", "arbitrary"), vm),
        name="w_in_relayout",
    )(w_in)


def _prep_weights(w_in, mla_w_q_up, mla_w_kv_up, mla_q_head_g, mla_k_head_g, gla_w_gate_up):
    L = w_in.shape[0]
    o = np.cumsum((0, MLA_LORA, MLA_ROPE, GLA_HEADS * GLA_DK, GLA_HEADS * GLA_DV, GLA_GATE_RANK,
                   GLA_GATE_RANK, MLA_LORA, GLA_HEADS * GLA_DK, GLA_HEADS * GLA_DV, BRANCH_DIM))
    ckv, kr, gk, gv, lrf, lrb, cq, gq, gg, pin, gate = (slice(int(a), int(b)) for a, b in
                                                        zip(o, list(o[1:]) + [w_in.shape[2]]))
    w_main = _relayout_call(w_in, (ckv, cq, gk, gv, gq, gg, pin, gate))
    d = w_in.shape[1]
    w_tail = jnp.concatenate([w_in[..., kr], w_in[..., lrf], w_in[..., lrb],
                              jnp.zeros((L, d, TAIL_COLS - MLA_ROPE - 2 * GLA_GATE_RANK), w_in.dtype)],
                             axis=-1).astype(BF16)
    wq = mla_w_q_up.reshape(L, MLA_LORA, MLA_HEADS, MLA_QK)
    wq_rope = jnp.pad(wq[..., MLA_NOPE:], ((0, 0), (0, 0), (0, 0), (0, V7X_LANES - MLA_ROPE)))
    wq_p = jnp.concatenate([wq[..., :MLA_NOPE].reshape(L, MLA_LORA, -1),
                            wq_rope.reshape(L, MLA_LORA, -1)], axis=-1).astype(BF16)
    wkv = mla_w_kv_up.reshape(L, MLA_LORA, MLA_HEADS, MLA_NOPE + MLA_V)
    wkv_p = jnp.concatenate([wkv[..., :MLA_NOPE].reshape(L, MLA_LORA, -1),
                             wkv[..., MLA_NOPE:].reshape(L, MLA_LORA, -1)], axis=-1).astype(BF16)

    def split_gain(g):
        return (g[:, None, :MLA_NOPE],
                jnp.pad(g[:, None, MLA_NOPE:], ((0, 0), (0, 0), (0, V7X_LANES - MLA_ROPE))))

    qgn, qgr = split_gain(mla_q_head_g)
    kgn, kgr = split_gain(mla_k_head_g)
    wg = jnp.zeros((L, 2, TAIL_COLS, GLA_HEADS * GLA_DK), BF16)
    wg = wg.at[:, 0, TAIL_LRF:TAIL_LRF + GLA_GATE_RANK].set(gla_w_gate_up[:, 0].astype(BF16))
    wg = wg.at[:, 1, TAIL_LRB:TAIL_LRB + GLA_GATE_RANK].set(gla_w_gate_up[:, 1].astype(BF16))
    return w_main, w_tail, wq_p, wkv_p, (qgn, qgr, kgn, kgr), wg


def kernel(x, c, ctx, c_ctx, w_mod, b_mod, norm1_g, norm2_g, w_in, mla_q_norm_g, mla_w_q_up,
           mla_kv_norm_g, mla_w_kv_up, mla_q_head_g, mla_k_head_g, gla_w_gate_up, gla_b_gate,
           gla_out_norm_g, pool_w, pool_scale, w_branch, w_out, ffn_w1, ffn_w3, ffn_w2):
    batch, seq, d = x.shape
    n_ctx = ctx.shape[1]
    depth = w_mod.shape[0]
    assert batch + 1 <= 8 and seq % GLA_BLOCK == 0 and n_ctx % GLA_BLOCK == 0

    c8 = jnp.zeros((8, d), F32).at[:batch].set(c).at[batch].set(c_ctx)
    mods_all = _mods_call(c8, w_mod, b_mod).reshape(depth, 8 * N_MOD, 1, d)

    w_main, w_tail, wq_p, wkv_p, (qgn, qgr, kgn, kgr), wg = _prep_weights(
        w_in, mla_w_q_up, mla_w_kv_up, mla_q_head_g, mla_k_head_g, gla_w_gate_up)
    w_branch_b = w_branch.astype(BF16)
    w_out_b = w_out.astype(BF16)
    w1_b, w3_b, w2_b = ffn_w1.astype(BF16), ffn_w3.astype(BF16), ffn_w2.astype(BF16)
    pool_w_b = pool_w.astype(BF16)
    rope_tabs = _rope_tables(seq)
    gla_consts = _gla_constants()
    bands = _pool_bands()
    n_main = w_main.shape[2]

    tok_l = _Tokens(batch * seq, seq, 0)
    tok_c = _Tokens(batch * n_ctx, None, batch)
    xl = x.reshape(batch * seq, d)
    xc = ctx.reshape(batch * n_ctx, d)

    for l in range(depth):
        last = l == depth - 1
        mods = mods_all[l]
        g1 = norm1_g[l][None, :]
        g2 = norm2_g[l][None, :]
        gains = (mla_q_norm_g[l][None, :], mla_kv_norm_g[l][None, :], qgn[l], qgr[l], kgn[l], kgr[l])

        z_l = _norm_matmul_call(tok_l, xl, g1, mods, 0, w_main, l, n_main, 1536, BF16, "in_proj")
        zt_l = _norm_matmul_call(tok_l, xl, g1, mods, 0, w_tail, l, TAIL_COLS, TAIL_COLS, F32, "in_tail")
        z_c = _norm_matmul_call(tok_c, xc, g1, mods, 0, w_main, l, Z_KV_END if last else n_main,
                                1536, BF16, "in_proj_ctx")
        zt_c = _norm_matmul_call(tok_c, xc, g1, mods, 0, w_tail, l, TAIL_COLS, TAIL_COLS, F32, "in_tail_ctx")

        q_l, k_l, v_l = _mla_prep_call(z_l, zt_l, wq_p, wkv_p, l, gains, rope_tabs, seq)
        q_c, k_c, v_c = _mla_prep_call(z_c, zt_c, wq_p, wkv_p, l, gains, None, n_ctx)
        mla_l = _attn_lat_call(q_l, k_l, v_l, k_c, v_c, batch, seq, n_ctx)

        gla_l, gla_c = _gla_call(z_l, zt_l, z_c, zt_c, wg[l], gla_b_gate[l][:, None, :],
                                 gla_out_norm_g[l][None, :], gla_consts, batch, seq, n_ctx, not last)
        pool_l, pool_c = _pool_call(z_l, None if last else z_c, bands, pool_w_b[l],
                                    pool_scale[l][None, :], batch, seq, n_ctx)

        def finish_layer(tok, xs, z, pool_o, mla_o, gla_o):
            merged = _merge_call(tok, pool_o, mla_o, gla_o, z, w_branch_b, l, d)
            xs = _matmul_residual_call(tok, merged, w_out_b, l, xs, mods, 2, 1024, "out_proj")
            up = _swiglu_up_call(tok, xs, g2, mods, w1_b, w3_b, l)
            return _matmul_residual_call(tok, up, w2_b, l, xs, mods, 5, 512, "swiglu_down")

        xl = finish_layer(tok_l, xl, z_l, pool_l, mla_l, gla_l)
        if not last:
            mla_c = _attn_ctx_call(q_c, k_c, v_c, batch, n_ctx)
            xc = finish_layer(tok_c, xc, z_c, pool_c, mla_c, gla_c)
    return xl.reshape(batch, seq, d)
```

```python
import functools

import numpy as np
import jax
import jax.numpy as jnp
from jax import lax
from jax.experimental import pallas as pl
from jax.experimental.pallas import tpu as pltpu

F32 = jnp.float32
BF16 = jnp.bfloat16

GRID_W = 64
RMS_EPS = 1e-6
N_BRANCH = 3
BRANCH_DIM = 1024
POOL_WINDOWS = (2, 4, 8, 16)
N_POOL_GROUPS = len(POOL_WINDOWS)
POOL_GROUP = BRANCH_DIM // N_POOL_GROUPS
MLA_HEADS = 8
MLA_LORA = 512
MLA_NOPE = 128
MLA_ROPE = 64
MLA_V = BRANCH_DIM // MLA_HEADS
MLA_QK = MLA_NOPE + MLA_ROPE
ROPE_BASE = 10000.0
GLA_HEADS = 4
GLA_DK = 256
GLA_DV = BRANCH_DIM // GLA_HEADS
GLA_GATE_RANK = 16
GLA_GATE_TAU = 16.0
N_MOD = 6

V7X_VMEM_BYTES = 64 * 1024 * 1024
V7X_LANES = 128
MXU_DIM = 256

MLA_HEAD_PAD = 2 * V7X_LANES
GLA_BLOCK = 256
GLA_SPANS = (128, 64, 32, 16)
GLA_SUB = 16
POOL_HALO = 16
TAIL_COLS = V7X_LANES
TAIL_LRF = MLA_ROPE
TAIL_LRB = MLA_ROPE + GLA_GATE_RANK

Z_CKV = 0
Z_CQ = Z_CKV + MLA_LORA
Z_GK = Z_CQ + MLA_LORA
Z_GV = Z_GK + GLA_HEADS * GLA_DK
Z_GQ = Z_GV + GLA_HEADS * GLA_DV
Z_GG = Z_GQ + GLA_HEADS * GLA_DK
Z_PIN = Z_GG + GLA_HEADS * GLA_DV
Z_GATE = Z_PIN + BRANCH_DIM
Z_KV_END = Z_GQ


def _divisor_tile(n, pref, align):
    if n <= pref:
        return n
    best = None
    for t in range(align, pref + 1, align):
        if n % t == 0:
            best = t
    assert best is not None, (n, pref, align)
    return best


def _params(semantics, vmem_bytes):
    limit = int(min(V7X_VMEM_BYTES - 6 * 1024 * 1024, max(vmem_bytes, 16 * 1024 * 1024)))
    return pltpu.CompilerParams(dimension_semantics=semantics, vmem_limit_bytes=limit)


def _nbytes(shape, dtype):
    return int(np.prod(shape)) * jnp.dtype(dtype).itemsize


def _sigmoid(x):
    return 1.0 / (1.0 + jnp.exp(-x))


def _k_mods(c_ref, w_ref, b_ref, o_ref):
    cv = c_ref[...]
    a = cv * _sigmoid(cv)
    o_ref[...] = jnp.dot(a, w_ref[...], precision=lax.Precision.HIGHEST,
                         preferred_element_type=F32) + b_ref[...]


def _mods_call(c8, w_mod, b_mod):
    L, D, N = w_mod.shape
    tn = _divisor_tile(N, 2048, V7X_LANES)
    vm = 2 * _nbytes((D, tn), F32) + 4 * _nbytes((8, tn), F32) + 2 * _nbytes((8, D), F32) + (4 << 20)
    return pl.pallas_call(
        _k_mods,
        grid=(L, N // tn),
        in_specs=[pl.BlockSpec((8, D), lambda l, j: (0, 0)),
                  pl.BlockSpec((None, D, tn), lambda l, j: (l, 0, j)),
                  pl.BlockSpec((None, 1, tn), lambda l, j: (l, 0, j))],
        out_specs=pl.BlockSpec((None, 8, tn), lambda l, j: (l, 0, j)),
        out_shape=jax.ShapeDtypeStruct((L, 8, N), F32),
        compiler_params=_params(("arbitrary", "arbitrary"), vm),
        name="mods",
    )(c8, w_mod, b_mod.reshape(L, 1, N))


NORM_ROWS = 64


def _norm_mod_rows(x_ref, g_ref, sh_ref, sc_ref, h_ref):
    tm = x_ref.shape[0]
    step = min(NORM_ROWS, tm)
    gain = g_ref[...] * (1.0 + sc_ref[...])
    sh = sh_ref[...]

    def body(r, carry):
        rows = pl.ds(pl.multiple_of(r * step, step), step)
        x = x_ref[rows, :]
        ms = jnp.mean(x * x, axis=-1, keepdims=True)
        h_ref[rows, :] = (x * lax.rsqrt(ms + RMS_EPS) * gain + sh).astype(h_ref.dtype)
        return carry

    lax.fori_loop(0, tm // step, body, 0)


def _k_norm_matmul(x_ref, g_ref, sh_ref, sc_ref, w_ref, wt_ref, o_ref, ot_ref, h_scr):
    @pl.when(pl.program_id(1) == 0)
    def _():
        _norm_mod_rows(x_ref, g_ref, sh_ref, sc_ref, h_scr)
        ot_ref[...] = lax.dot_general(h_scr[...], wt_ref[...].astype(h_scr.dtype), _NT,
                                      preferred_element_type=F32)

    o_ref[...] = lax.dot_general(h_scr[...], w_ref[...], _NT,
                                 preferred_element_type=F32).astype(o_ref.dtype)


def _k_norm_swiglu_up(x_ref, g_ref, sh_ref, sc_ref, w1_ref, w3_ref, o_ref, h_scr):
    @pl.when(pl.program_id(1) == 0)
    def _():
        _norm_mod_rows(x_ref, g_ref, sh_ref, sc_ref, h_scr)

    h = h_scr[...]
    a = jnp.dot(h, w1_ref[...], preferred_element_type=F32)
    b = jnp.dot(h, w3_ref[...], preferred_element_type=F32)
    o_ref[...] = (a * _sigmoid(a) * b).astype(o_ref.dtype)


def _k_matmul_residual(a_ref, w_ref, x_ref, gate_ref, o_ref):
    y = jnp.dot(a_ref[...], w_ref[...], preferred_element_type=F32)
    o_ref[...] = x_ref[...] + gate_ref[...] * y


def _k_merge(p_ref, m_ref, gl_ref, g0_ref, g1_ref, g2_ref, wb_ref, o_ref):
    acc = None
    for n, (br, gt) in enumerate(((p_ref, g0_ref), (m_ref, g1_ref), (gl_ref, g2_ref))):
        proj = jnp.dot(br[...], wb_ref[n], preferred_element_type=F32)
        term = _sigmoid(gt[...].astype(F32)) * proj
        acc = term if acc is None else acc + term
    o_ref[...] = acc.astype(o_ref.dtype)


class _Tokens:
    def __init__(self, n_rows, rows_per_mod, mod_row_base, tm_pref=1024):
        self.m = n_rows
        if rows_per_mod is None:
            self.tm = _divisor_tile(n_rows, tm_pref, 8)
            self.mod_row = lambda i: mod_row_base
        else:
            self.tm = _divisor_tile(rows_per_mod, tm_pref, 8)
            per = rows_per_mod // self.tm
            self.mod_row = lambda i: mod_row_base + i // per
        self.n_tiles = n_rows // self.tm

    def mod_spec(self, k, d, tn=None):
        if tn is None:
            return pl.BlockSpec((None, 1, d), lambda i, j: (self.mod_row(i) * N_MOD + k, 0, 0))
        return pl.BlockSpec((None, 1, tn), lambda i, j: (self.mod_row(i) * N_MOD + k, 0, j))


def _layer_spec(layer, block, col_axis_index):
    zeros = (0,) * (len(block) - 1)
    return pl.BlockSpec((None,) + block, lambda i, j: (layer,) + zeros + (col_axis_index(j),))


def _in_proj_call(tok, x, g, mods, w, w_tail, layer, n_cols, tn_pref, name):
    m, d = x.shape
    tm = tok.tm
    tn = _divisor_tile(n_cols, tn_pref, V7X_LANES)
    n_tail = w_tail.shape[1]
    vm = (2 * _nbytes((tm, d), F32) + _nbytes((tm, d), BF16) + 2 * _nbytes((d, tn + n_tail), BF16)
          + 2 * _nbytes((tm, tn), BF16) + 2 * _nbytes((tm, tn + n_tail), F32) + (4 << 20))
    return pl.pallas_call(
        _k_norm_matmul,
        grid=(tok.n_tiles, n_cols // tn),
        in_specs=[pl.BlockSpec((tm, d), lambda i, j: (i, 0)),
                  pl.BlockSpec((1, d), lambda i, j: (0, 0)),
                  tok.mod_spec(0, d),
                  tok.mod_spec(1, d),
                  pl.BlockSpec((None, tn, d), lambda i, j: (layer, j, 0)),
                  pl.BlockSpec((None, n_tail, d), lambda i, j: (layer, 0, 0))],
        out_specs=[pl.BlockSpec((tm, tn), lambda i, j: (i, j)),
                   pl.BlockSpec((tm, n_tail), lambda i, j: (i, 0))],
        out_shape=[jax.ShapeDtypeStruct((m, n_cols), BF16),
                   jax.ShapeDtypeStruct((m, n_tail), F32)],
        scratch_shapes=[pltpu.VMEM((tm, d), BF16)],
        compiler_params=_params(("arbitrary", "arbitrary"), vm),
        name=name,
    )(x, g, mods, mods, w, w_tail)


def _swiglu_up_call(tok, x, g, mods, w1, w3, layer):
    m, d = x.shape
    f = w1.shape[2]
    tm = tok.tm
    tn = _divisor_tile(f, 512, V7X_LANES)
    vm = (2 * _nbytes((tm, d), F32) + _nbytes((tm, d), BF16) + 4 * _nbytes((d, tn), BF16)
          + 2 * _nbytes((tm, tn), BF16) + 4 * _nbytes((tm, tn), F32) + (4 << 20))
    return pl.pallas_call(
        _k_norm_swiglu_up,
        grid=(tok.n_tiles, f // tn),
        in_specs=[pl.BlockSpec((tm, d), lambda i, j: (i, 0)),
                  pl.BlockSpec((1, d), lambda i, j: (0, 0)),
                  tok.mod_spec(3, d),
                  tok.mod_spec(4, d),
                  _layer_spec(layer, (d, tn), lambda j: j),
                  _layer_spec(layer, (d, tn), lambda j: j)],
        out_specs=pl.BlockSpec((tm, tn), lambda i, j: (i, j)),
        out_shape=jax.ShapeDtypeStruct((m, f), BF16),
        scratch_shapes=[pltpu.VMEM((tm, d), BF16)],
        compiler_params=_params(("arbitrary", "arbitrary"), vm),
        name="swiglu_up",
    )(x, g, mods, mods, w1, w3)


def _matmul_residual_call(tok, a, w, layer, x, mods, k_gate, tn_pref, name):
    m, kdim = a.shape
    d = w.shape[2]
    tm = tok.tm
    tn = _divisor_tile(d, tn_pref, V7X_LANES)
    vm = (2 * _nbytes((tm, kdim), BF16) + 2 * _nbytes((kdim, tn), BF16)
          + 5 * _nbytes((tm, tn), F32) + (4 << 20))
    return pl.pallas_call(
        _k_matmul_residual,
        grid=(tok.n_tiles, d // tn),
        in_specs=[pl.BlockSpec((tm, kdim), lambda i, j: (i, 0)),
                  _layer_spec(layer, (kdim, tn), lambda j: j),
                  pl.BlockSpec((tm, tn), lambda i, j: (i, j)),
                  tok.mod_spec(k_gate, d, tn)],
        out_specs=pl.BlockSpec((tm, tn), lambda i, j: (i, j)),
        out_shape=jax.ShapeDtypeStruct((m, d), F32),
        compiler_params=_params(("arbitrary", "arbitrary"), vm),
        name=name,
    )(a, w, x, mods)


def _merge_call(tok, pool_o, mla_o, gla_o, z, w_branch, layer, d):
    m = pool_o.shape[0]
    tm = tok.tm
    tn = _divisor_tile(d, 512, V7X_LANES)
    assert Z_GATE % tn == 0
    g0 = Z_GATE // tn
    per = d // tn
    br_spec = pl.BlockSpec((tm, BRANCH_DIM), lambda i, j: (i, 0))
    vm = (6 * _nbytes((tm, BRANCH_DIM), BF16) + 6 * _nbytes((tm, tn), BF16)
          + 2 * _nbytes((N_BRANCH, BRANCH_DIM, tn), BF16) + 2 * _nbytes((tm, tn), BF16)
          + 4 * _nbytes((tm, tn), F32) + (4 << 20))
    return pl.pallas_call(
        _k_merge,
        grid=(tok.n_tiles, per),
        in_specs=[br_spec, br_spec, br_spec,
                  pl.BlockSpec((tm, tn), lambda i, j: (i, g0 + j)),
                  pl.BlockSpec((tm, tn), lambda i, j: (i, g0 + per + j)),
                  pl.BlockSpec((tm, tn), lambda i, j: (i, g0 + 2 * per + j)),
                  _layer_spec(layer, (N_BRANCH, BRANCH_DIM, tn), lambda j: j)],
        out_specs=pl.BlockSpec((tm, tn), lambda i, j: (i, j)),
        out_shape=jax.ShapeDtypeStruct((m, d), BF16),
        compiler_params=_params(("arbitrary", "arbitrary"), vm),
        name="merge",
    )(pool_o, mla_o, gla_o, z, z, z, w_branch)


def _k_mla_prep(use_rope, zkv_ref, zq_ref, zt_ref, wq_ref, wkv_ref, qng_ref, kvng_ref,
                qgn_ref, qgr_ref, kgn_ref, kgr_ref, *rest):
    if use_rope:
        c_ref, sa_ref, sb_ref, q_out, k_out, v_out = rest
        rc, rsa, rsb = c_ref[...], sa_ref[...], sb_ref[...]
    else:
        q_out, k_out, v_out = rest
    hn = MLA_HEADS * MLA_NOPE

    def rms(x, g):
        xf = x.astype(F32)
        return xf * lax.rsqrt(jnp.mean(xf * xf, axis=-1, keepdims=True) + RMS_EPS) * g

    def rope(x):
        if not use_rope:
            return x
        return (x * rc + pltpu.roll(x, V7X_LANES - MLA_ROPE // 4, 1) * rsa
                + pltpu.roll(x, MLA_ROPE // 4, 1) * rsb)

    q = jnp.dot(rms(zq_ref[...], qng_ref[...]).astype(BF16), wq_ref[...], preferred_element_type=F32)
    kv = jnp.dot(rms(zkv_ref[...], kvng_ref[...]).astype(BF16), wkv_ref[...], preferred_element_type=F32)
    v_out[...] = kv[:, hn:].astype(v_out.dtype)

    zt = zt_ref[...]
    lane = lax.broadcasted_iota(jnp.int32, zt.shape, 1)
    kr = jnp.where(lane < MLA_ROPE, zt, 0.0)
    kr_ss = jnp.sum(kr * kr, axis=-1, keepdims=True)
    qgn, qgr, kgn, kgr = qgn_ref[...], qgr_ref[...], kgn_ref[...], kgr_ref[...]
    inv_dim = 1.0 / MLA_QK
    q_scale = MLA_QK ** -0.5
    for h in range(MLA_HEADS):
        lo = h * MLA_HEAD_PAD
        qn = q[:, h * MLA_NOPE:(h + 1) * MLA_NOPE]
        qr = q[:, hn + h * V7X_LANES: hn + (h + 1) * V7X_LANES]
        ss = jnp.sum(qn * qn, axis=-1, keepdims=True) + jnp.sum(qr * qr, axis=-1, keepdims=True)
        r = lax.rsqrt(ss * inv_dim + RMS_EPS) * q_scale
        q_out[:, lo:lo + MLA_NOPE] = (qn * r * qgn).astype(q_out.dtype)
        q_out[:, lo + MLA_NOPE:lo + MLA_HEAD_PAD] = rope(qr * r * qgr).astype(q_out.dtype)
        kn = kv[:, h * MLA_NOPE:(h + 1) * MLA_NOPE]
        rk = lax.rsqrt((jnp.sum(kn * kn, axis=-1, keepdims=True) + kr_ss) * inv_dim + RMS_EPS)
        k_out[:, lo:lo + MLA_NOPE] = (kn * rk * kgn).astype(k_out.dtype)
        k_out[:, lo + MLA_NOPE:lo + MLA_HEAD_PAD] = rope(kr * rk * kgr).astype(k_out.dtype)


def _mla_prep_call(z, zt, wq, wkv, layer, gains, rope_tabs, seq_len):
    m = z.shape[0]
    tm = _divisor_tile(seq_len, 256, 16)
    use_rope = rope_tabs is not None
    hq = MLA_HEADS * MLA_HEAD_PAD
    hv = MLA_HEADS * MLA_V
    full = lambda shape: pl.BlockSpec(shape, lambda i: tuple(0 for _ in shape))
    in_specs = [pl.BlockSpec((tm, MLA_LORA), lambda i: (i, Z_CKV // MLA_LORA)),
                pl.BlockSpec((tm, MLA_LORA), lambda i: (i, Z_CQ // MLA_LORA)),
                pl.BlockSpec((tm, TAIL_COLS), lambda i: (i, 0)),
                pl.BlockSpec((None,) + wq.shape[1:], lambda i: (layer, 0, 0)),
                pl.BlockSpec((None,) + wkv.shape[1:], lambda i: (layer, 0, 0))]
    in_specs += [full(g.shape) for g in gains]
    args = [z, z, zt, wq, wkv, *gains]
    if use_rope:
        per = seq_len // tm
        in_specs += [pl.BlockSpec((tm, V7X_LANES), lambda i: (i % per, 0))] * 3
        args += list(rope_tabs)
    vm = (8 * _nbytes((tm, hq), F32) + 4 * _nbytes(wq.shape[1:], BF16) + (8 << 20))
    return pl.pallas_call(
        functools.partial(_k_mla_prep, use_rope),
        grid=(m // tm,),
        in_specs=in_specs,
        out_specs=[pl.BlockSpec((tm, hq), lambda i: (i, 0)),
                   pl.BlockSpec((tm, hq), lambda i: (i, 0)),
                   pl.BlockSpec((tm, hv), lambda i: (i, 0))],
        out_shape=[jax.ShapeDtypeStruct((m, hq), BF16),
                   jax.ShapeDtypeStruct((m, hq), BF16),
                   jax.ShapeDtypeStruct((m, hv), BF16)],
        compiler_params=_params(("arbitrary",), vm),
        name="mla_prep_rope" if use_rope else "mla_prep",
    )(*args)


_NT = (((1,), (1,)), ((), ()))
_TN = (((0,), (0,)), ((), ()))


ATTN_HEADS_PER_STEP = 4
ATTN_Q_ROWS = 256


def _k_attn_lat(q_ref, kl_ref, kc_ref, vl_ref, vc_ref, o_ref):
    for h in range(ATTN_HEADS_PER_STEP):
        qk = slice(h * MLA_HEAD_PAD, (h + 1) * MLA_HEAD_PAD)
        vv = slice(h * MLA_V, (h + 1) * MLA_V)
        q = q_ref[:, qk]
        sl = lax.dot_general(q, kl_ref[:, qk], _NT, preferred_element_type=F32)
        sc = lax.dot_general(q, kc_ref[:, qk], _NT, preferred_element_type=F32)
        mx = jnp.maximum(jnp.max(sl, axis=-1, keepdims=True), jnp.max(sc, axis=-1, keepdims=True))
        p_l = jnp.exp(sl - mx)
        p_c = jnp.exp(sc - mx)
        den = jnp.sum(p_l, axis=-1, keepdims=True) + jnp.sum(p_c, axis=-1, keepdims=True)
        o = (jnp.dot(p_c.astype(BF16), vc_ref[:, vv], preferred_element_type=F32)
             + jnp.dot(p_l.astype(BF16), vl_ref[:, vv], preferred_element_type=F32))
        o_ref[:, vv] = (o / den).astype(o_ref.dtype)


def _k_attn_ctx(q_ref, kc_ref, vc_ref, o_ref):
    for h in range(ATTN_HEADS_PER_STEP):
        qk = slice(h * MLA_HEAD_PAD, (h + 1) * MLA_HEAD_PAD)
        vv = slice(h * MLA_V, (h + 1) * MLA_V)
        sc = lax.dot_general(q_ref[:, qk], kc_ref[:, qk], _NT, preferred_element_type=F32)
        p_c = jnp.exp(sc - jnp.max(sc, axis=-1, keepdims=True))
        den = jnp.sum(p_c, axis=-1, keepdims=True)
        o = jnp.dot(p_c.astype(BF16), vc_ref[:, vv], preferred_element_type=F32)
        o_ref[:, vv] = (o / den).astype(o_ref.dtype)


def _attn_lat_call(q_l, k_l, v_l, k_c, v_c, batch, seq, ctx):
    tq = _divisor_tile(seq, ATTN_Q_ROWS, 16)
    nq = seq // tq
    hp = ATTN_HEADS_PER_STEP
    wqk, wv = hp * MLA_HEAD_PAD, hp * MLA_V
    vm = (4 * hp * _nbytes((tq, seq + ctx), F32) + 2 * _nbytes((seq + ctx, wqk + wv), BF16) + (8 << 20))
    return pl.pallas_call(
        _k_attn_lat,
        grid=(batch, MLA_HEADS // hp, nq),
        in_specs=[pl.BlockSpec((tq, wqk), lambda b, h, i: (b * nq + i, h)),
                  pl.BlockSpec((seq, wqk), lambda b, h, i: (b, h)),
                  pl.BlockSpec((ctx, wqk), lambda b, h, i: (b, h)),
                  pl.BlockSpec((seq, wv), lambda b, h, i: (b, h)),
                  pl.BlockSpec((ctx, wv), lambda b, h, i: (b, h))],
        out_specs=pl.BlockSpec((tq, wv), lambda b, h, i: (b * nq + i, h)),
        out_shape=jax.ShapeDtypeStruct((batch * seq, MLA_HEADS * MLA_V), BF16),
        compiler_params=_params(("arbitrary", "arbitrary", "arbitrary"), vm),
        name="attn_lat",
    )(q_l, k_l, k_c, v_l, v_c)


def _attn_ctx_call(q_c, k_c, v_c, batch, ctx):
    hp = ATTN_HEADS_PER_STEP
    wqk, wv = hp * MLA_HEAD_PAD, hp * MLA_V
    vm = 4 * hp * _nbytes((ctx, ctx), F32) + (8 << 20)
    return pl.pallas_call(
        _k_attn_ctx,
        grid=(batch, MLA_HEADS // hp),
        in_specs=[pl.BlockSpec((ctx, wqk), lambda b, h: (b, h)),
                  pl.BlockSpec((ctx, wqk), lambda b, h: (b, h)),
                  pl.BlockSpec((ctx, wv), lambda b, h: (b, h))],
        out_specs=pl.BlockSpec((ctx, wv), lambda b, h: (b, h)),
        out_shape=jax.ShapeDtypeStruct((batch * ctx, MLA_HEADS * MLA_V), BF16),
        compiler_params=_params(("arbitrary", "arbitrary"), vm),
        name="attn_ctx",
    )(q_c, k_c, v_c)


def _gla_constants():
    c, sub = GLA_BLOCK, GLA_SUB
    t = np.arange(c)[:, None]
    s = np.arange(c)[None, :]
    tris, masks = [], []
    for d in range(2):
        step = -1 if d == 0 else 1
        tris.append(s <= t if d == 0 else s >= t)
        lv = [(((t // sp) % 2) == (1 if d == 0 else 0)) & (s // sp == t // sp + step)
              for sp in GLA_SPANS]
        lv.append((t // sub == s // sub) & (s <= t if d == 0 else s >= t))
        assert (np.sum(lv, axis=0) == tris[-1]).all()
        masks.append(np.stack(lv, 0))
    return (jnp.asarray(np.stack(tris, 0), BF16), jnp.asarray(np.stack(masks, 0), F32))


def _gla_ref_rows(cum, d, span):
    c = cum.shape[0]
    pieces = []
    if span is None:
        for g in range(c // GLA_SUB):
            idx = g * GLA_SUB - 1 if d == 0 else (g + 1) * GLA_SUB
            row = cum[idx:idx + 1] if 0 <= idx < c else jnp.zeros_like(cum[0:1])
            pieces.append(jnp.broadcast_to(row, (GLA_SUB, cum.shape[1])))
    else:
        for g in range(c // (2 * span)):
            idx = g * 2 * span + (span - 1 if d == 0 else span)
            pieces.append(jnp.broadcast_to(cum[idx:idx + 1], (2 * span, cum.shape[1])))
    return pieces[0] if len(pieces) == 1 else jnp.concatenate(pieces, axis=0)


def _k_gla(need_ctx_out, ql_ref, kl_ref, vl_ref, ggl_ref, ztl_ref, kc_ref, vc_ref, ztc_ref,
           wg_ref, bg_ref, gn_ref, tri_ref, masks_ref, *rest):
    if need_ctx_out:
        qc_ref, ggc_ref, ol_ref, oc_ref, acc_l, acc_c, st_ref = rest
    else:
        ol_ref, acc_l, st_ref = rest
        qc_ref = ggc_ref = oc_ref = acc_c = None
    c = GLA_BLOCK
    q_scale = GLA_DK ** -0.5
    log2e = float(np.log2(np.e))
    inv_tau = 1.0 / GLA_GATE_TAU

    def by_role(d, span, q, k):
        pieces = [(q if (g % 2 == 1) == (d == 0) else k)[g * span:(g + 1) * span]
                  for g in range(c // span)]
        return jnp.concatenate(pieces, axis=0)

    def chunk(d, q_ref, k_ref, v_ref, zt_ref, j, acc_ref):
        rows = pl.ds(pl.multiple_of(j * c, c), c)
        k = k_ref[rows, :].astype(F32)
        v = v_ref[rows, :]
        zg = jnp.dot(zt_ref[rows, :].astype(BF16), wg_ref[d], preferred_element_type=F32) + bg_ref[d]
        softplus = jnp.log(1.0 + jnp.exp2(jnp.abs(zg) * (-log2e)))
        la = (jnp.minimum(zg, 0.0) - softplus) * (inv_tau * log2e)
        hi = la.astype(BF16)
        lo = (la - hi.astype(F32)).astype(BF16)
        tri = tri_ref[d]
        cum = (jnp.dot(tri, hi, preferred_element_type=F32) + jnp.dot(tri, lo, preferred_element_type=F32))
        tot = cum[c - 1:c] if d == 0 else cum[0:1]
        st = st_ref[d]
        if acc_ref is not None:
            q = q_ref[rows, :].astype(F32)
            inter = lax.dot_general((q * jnp.exp2(cum)).astype(BF16), st.astype(BF16), _NT,
                                    preferred_element_type=F32)
            a = None
            for lvl, span in enumerate(GLA_SPANS + (None,)):
                dlt = cum - _gla_ref_rows(cum, d, span)
                if span is None:
                    qe = (q * jnp.exp2(dlt)).astype(BF16)
                    ke = (k * jnp.exp2(-dlt)).astype(BF16)
                else:
                    qe = ke = (by_role(d, span, q, k) * jnp.exp2(-jnp.abs(dlt))).astype(BF16)
                part = masks_ref[d, lvl] * lax.dot_general(qe, ke, _NT, preferred_element_type=F32)
                a = part if a is None else a + part
            intra = jnp.dot(a.astype(BF16), v, preferred_element_type=F32)
            acc_ref[rows, :] += (inter + intra) * q_scale
        kd = (k * jnp.exp2(tot - cum)).astype(BF16)
        st_ref[d] = st * jnp.exp2(tot) + lax.dot_general(v, kd, _TN, preferred_element_type=F32)

    def scan(q_ref, k_ref, v_ref, zt_ref, acc_ref):
        n = k_ref.shape[0] // c
        if acc_ref is not None:
            acc_ref[...] = jnp.zeros(acc_ref.shape, acc_ref.dtype)

        def body(i, carry):
            chunk(0, q_ref, k_ref, v_ref, zt_ref, i, acc_ref)
            chunk(1, q_ref, k_ref, v_ref, zt_ref, n - 1 - i, acc_ref)
            return carry

        lax.fori_loop(0, n, body, 0)

    def finish(acc_ref, gg_ref, o_ref):
        n_rows = acc_ref.shape[0]
        step = min(NORM_ROWS, n_rows)
        gn = gn_ref[...]

        def body(r, carry):
            rows = pl.ds(pl.multiple_of(r * step, step), step)
            o = acc_ref[rows, :]
            y = o * lax.rsqrt(jnp.mean(o * o, axis=-1, keepdims=True) + RMS_EPS) * gn
            g = gg_ref[rows, :].astype(F32)
            o_ref[rows, :] = (y * (g * _sigmoid(g))).astype(o_ref.dtype)
            return carry

        lax.fori_loop(0, n_rows // step, body, 0)

    st_ref[...] = jnp.zeros(st_ref.shape, st_ref.dtype)
    scan(qc_ref, kc_ref, vc_ref, ztc_ref, acc_c)
    scan(ql_ref, kl_ref, vl_ref, ztl_ref, acc_l)
    finish(acc_l, ggl_ref, ol_ref)
    if need_ctx_out:
        finish(acc_c, ggc_ref, oc_ref)


def _gla_call(z_l, zt_l, z_c, zt_c, wg, bg, gn, consts, batch, seq, ctx, need_ctx_out):
    tri, masks = consts
    blk = lambda rows, col0: pl.BlockSpec((rows, GLA_DK), lambda b, h: (b, col0 // GLA_DK + h))
    tail = lambda rows: pl.BlockSpec((rows, TAIL_COLS), lambda b, h: (b, 0))
    full = lambda a: pl.BlockSpec(a.shape, lambda b, h: tuple(0 for _ in a.shape))
    in_specs = [blk(seq, Z_GQ), blk(seq, Z_GK), blk(seq, Z_GV), blk(seq, Z_GG), tail(seq),
                blk(ctx, Z_GK), blk(ctx, Z_GV), tail(ctx),
                pl.BlockSpec((2, TAIL_COLS, GLA_DK), lambda b, h: (0, 0, h)),
                pl.BlockSpec((2, 1, GLA_DK), lambda b, h: (0, 0, h)),
                full(gn), full(tri), full(masks)]
    args = [z_l, z_l, z_l, z_l, zt_l, z_c, z_c, zt_c, wg, bg, gn, tri, masks]
    out_specs = [pl.BlockSpec((seq, GLA_DV), lambda b, h: (b, h))]
    out_shape = [jax.ShapeDtypeStruct((batch * seq, GLA_HEADS * GLA_DV), BF16)]
    scratch = [pltpu.VMEM((seq, GLA_DV), F32)]
    if need_ctx_out:
        in_specs += [blk(ctx, Z_GQ), blk(ctx, Z_GG)]
        args += [z_c, z_c]
        out_specs.append(pl.BlockSpec((ctx, GLA_DV), lambda b, h: (b, h)))
        out_shape.append(jax.ShapeDtypeStruct((batch * ctx, GLA_HEADS * GLA_DV), BF16))
        scratch.append(pltpu.VMEM((ctx, GLA_DV), F32))
    scratch.append(pltpu.VMEM((2, GLA_DV, GLA_DK), F32))
    vm = (10 * _nbytes((seq + ctx, GLA_DK), BF16) + 2 * _nbytes((seq + ctx, TAIL_COLS), F32)
          + _nbytes((seq + ctx, GLA_DV), F32) + (12 << 20))
    outs = pl.pallas_call(
        functools.partial(_k_gla, need_ctx_out),
        grid=(batch, GLA_HEADS),
        in_specs=in_specs,
        out_specs=out_specs,
        out_shape=out_shape,
        scratch_shapes=scratch,
        compiler_params=_params(("arbitrary", "arbitrary"), vm),
        name="gla" if need_ctx_out else "gla_last",
    )(*args)
    return (outs[0], outs[1]) if need_ctx_out else (outs[0], None)


POOL_ROWS = 256


def _pool_bands():
    t = np.arange(POOL_ROWS)[:, None]
    m = np.arange(POOL_ROWS + 2 * POOL_HALO)[None, :] - POOL_HALO
    bands = [(m >= t - w // 2) & (m <= t + w // 2 - 1) for w in POOL_WINDOWS]
    return jnp.asarray(np.stack(bands, 0), BF16)


def _k_pool(ul_ref, uc_ref, band_ref, w_ref, sc_ref, ol_ref, oc_ref, padl_ref, padc_ref):
    half = lax.shift_left(jnp.int32(1), pl.program_id(1))
    w = w_ref[...]
    scale = sc_ref[...]

    def run(u_ref, pad_ref, o_ref):
        n = u_ref.shape[0]
        rows = min(POOL_ROWS, n)
        zeros = jnp.zeros((POOL_HALO, pad_ref.shape[1]), pad_ref.dtype)
        pad_ref[0:POOL_HALO, :] = zeros
        pad_ref[POOL_HALO + n:2 * POOL_HALO + n, :] = zeros
        pad_ref[POOL_HALO:POOL_HALO + n, :] = u_ref[...]
        band = band_ref[0:rows, 0:rows + 2 * POOL_HALO]

        def body(r, carry):
            r0 = pl.multiple_of(r * rows, rows)
            win = jnp.dot(band, pad_ref[pl.ds(r0, rows + 2 * POOL_HALO), :],
                          preferred_element_type=F32)
            pos = r0 + lax.broadcasted_iota(jnp.int32, (rows, 1), 0)
            cnt = (jnp.minimum(pos, half) + jnp.minimum(n - pos, half)).astype(F32)
            pooled = win / cnt - u_ref[pl.ds(r0, rows), :].astype(F32)
            y = jnp.dot(pooled.astype(BF16), w, preferred_element_type=F32) * scale
            o_ref[pl.ds(r0, rows), :] = y.astype(o_ref.dtype)
            return carry

        lax.fori_loop(0, n // rows, body, 0)

    run(ul_ref, padl_ref, ol_ref)
    if oc_ref is not None:
        run(uc_ref, padc_ref, oc_ref)


def _k_pool_lat_only(ul_ref, band_ref, w_ref, sc_ref, ol_ref, padl_ref):
    _k_pool(ul_ref, None, band_ref, w_ref, sc_ref, ol_ref, None, padl_ref, None)


def _pool_call(z_l, z_c, bands, pool_w, pool_scale, batch, seq, ctx):
    g0 = Z_PIN // POOL_GROUP
    blk = lambda rows: pl.BlockSpec((rows, POOL_GROUP), lambda b, g: (b, g0 + g))
    oblk = lambda rows: pl.BlockSpec((rows, POOL_GROUP), lambda b, g: (b, g))
    common = [pl.BlockSpec((None,) + bands.shape[1:], lambda b, g: (g, 0, 0)),
              pl.BlockSpec((None, POOL_GROUP, POOL_GROUP), lambda b, g: (g, 0, 0)),
              pl.BlockSpec((1, POOL_GROUP), lambda b, g: (0, g))]
    vm = 8 * _nbytes((seq + ctx, POOL_GROUP), F32) + (8 << 20)
    pad = lambda rows: pltpu.VMEM((rows + 2 * POOL_HALO, POOL_GROUP), BF16)
    if z_c is None:
        out = pl.pallas_call(
            _k_pool_lat_only, grid=(batch, N_POOL_GROUPS),
            in_specs=[blk(seq)] + common, out_specs=oblk(seq),
            out_shape=jax.ShapeDtypeStruct((batch * seq, BRANCH_DIM), BF16),
            scratch_shapes=[pad(seq)],
            compiler_params=_params(("arbitrary", "arbitrary"), vm), name="pool_last",
        )(z_l, bands, pool_w, pool_scale)
        return out, None
    return pl.pallas_call(
        _k_pool, grid=(batch, N_POOL_GROUPS),
        in_specs=[blk(seq), blk(ctx)] + common, out_specs=[oblk(seq), oblk(ctx)],
        out_shape=[jax.ShapeDtypeStruct((batch * seq, BRANCH_DIM), BF16),
                   jax.ShapeDtypeStruct((batch * ctx, BRANCH_DIM), BF16)],
        scratch_shapes=[pad(seq), pad(ctx)],
        compiler_params=_params(("arbitrary", "arbitrary"), vm), name="pool",
    )(z_l, z_c, bands, pool_w, pool_scale)


def _rope_tables(seq):
    half = MLA_ROPE // 2
    quarter = half // 2
    inv_freq = 1.0 / (ROPE_BASE ** (jnp.arange(0, half, 2, dtype=F32) / half))
    t = jnp.arange(seq)
    ang_r = (t // GRID_W).astype(F32)[:, None] * inv_freq[None, :]
    ang_c = (t % GRID_W).astype(F32)[:, None] * inv_freq[None, :]
    cr, sr, cc, sn = jnp.cos(ang_r), jnp.sin(ang_r), jnp.cos(ang_c), jnp.sin(ang_c)
    z = jnp.zeros((seq, quarter), F32)
    zpad = jnp.zeros((seq, V7X_LANES - MLA_ROPE), F32)
    cos = jnp.concatenate([cr, cr, cc, cc, zpad], axis=1)
    sa = jnp.concatenate([-sr, z, -sn, z, zpad], axis=1)
    sb = jnp.concatenate([z, sr, z, sn, zpad], axis=1)
    return cos, sa, sb


REORDER_ROWS = 512
REORDER_ALIGN = 16


def _k_cast_rows(src_ref, w_ref, o_ref):
    del src_ref
    o_ref[...] = w_ref[0].astype(o_ref.dtype)


def _reorder_cast_call(w_t, groups):
    L, _, d = w_t.shape
    rows = int(np.gcd.reduce([REORDER_ROWS] + [s.stop - s.start for s in groups]))
    starts = []
    for s in groups:
        assert s.start % REORDER_ALIGN == 0 and rows % REORDER_ALIGN == 0
        starts += [r // REORDER_ALIGN for r in range(s.start, s.stop, rows)]
    n_chunks = len(starts)
    vm = 2 * _nbytes((rows, d), F32) + 2 * _nbytes((rows, d), BF16) + (4 << 20)
    grid_spec = pltpu.PrefetchScalarGridSpec(
        num_scalar_prefetch=1,
        grid=(L, n_chunks),
        in_specs=[pl.BlockSpec((pl.Element(1), pl.Element(rows), pl.Element(d)),
                               lambda l, c, src: (l, src[c] * REORDER_ALIGN, 0))],
        out_specs=pl.BlockSpec((None, rows, d), lambda l, c, src: (l, c, 0)),
    )
    return pl.pallas_call(
        _k_cast_rows,
        grid_spec=grid_spec,
        out_shape=jax.ShapeDtypeStruct((L, n_chunks * rows, d), BF16),
        compiler_params=_params(("arbitrary", "arbitrary"), vm),
        name="w_in_reorder",
    )(jnp.asarray(starts, jnp.int32), w_t)


def _prep_weights(w_in, mla_w_q_up, mla_w_kv_up, mla_q_head_g, mla_k_head_g, gla_w_gate_up):
    L = w_in.shape[0]
    o = np.cumsum((0, MLA_LORA, MLA_ROPE, GLA_HEADS * GLA_DK, GLA_HEADS * GLA_DV, GLA_GATE_RANK,
                   GLA_GATE_RANK, MLA_LORA, GLA_HEADS * GLA_DK, GLA_HEADS * GLA_DV, BRANCH_DIM))
    ckv, kr, gk, gv, lrf, lrb, cq, gq, gg, pin, gate = (slice(int(a), int(b)) for a, b in
                                                        zip(o, list(o[1:]) + [w_in.shape[2]]))
    w_t = jnp.swapaxes(w_in, 1, 2)
    w_main = _reorder_cast_call(w_t, (ckv, cq, gk, gv, gq, gg, pin, gate))
    tail_pad = jnp.zeros((L, TAIL_COLS - MLA_ROPE - 2 * GLA_GATE_RANK, w_in.shape[1]), w_in.dtype)
    w_tail = jnp.concatenate([w_t[:, s, :] for s in (kr, lrf, lrb)] + [tail_pad], axis=1)
    wq = mla_w_q_up.reshape(L, MLA_LORA, MLA_HEADS, MLA_QK)
    wq_rope = jnp.pad(wq[..., MLA_NOPE:], ((0, 0), (0, 0), (0, 0), (0, V7X_LANES - MLA_ROPE)))
    wq_p = jnp.concatenate([wq[..., :MLA_NOPE].reshape(L, MLA_LORA, -1),
                            wq_rope.reshape(L, MLA_LORA, -1)], axis=-1).astype(BF16)
    wkv = mla_w_kv_up.reshape(L, MLA_LORA, MLA_HEADS, MLA_NOPE + MLA_V)
    wkv_p = jnp.concatenate([wkv[..., :MLA_NOPE].reshape(L, MLA_LORA, -1),
                             wkv[..., MLA_NOPE:].reshape(L, MLA_LORA, -1)], axis=-1).astype(BF16)

    def split_gain(g):
        return (g[:, None, :MLA_NOPE],
                jnp.pad(g[:, None, MLA_NOPE:], ((0, 0), (0, 0), (0, V7X_LANES - MLA_ROPE))))

    qgn, qgr = split_gain(mla_q_head_g)
    kgn, kgr = split_gain(mla_k_head_g)
    wg = jnp.zeros((L, 2, TAIL_COLS, GLA_HEADS * GLA_DK), BF16)
    wg = wg.at[:, 0, TAIL_LRF:TAIL_LRF + GLA_GATE_RANK].set(gla_w_gate_up[:, 0].astype(BF16))
    wg = wg.at[:, 1, TAIL_LRB:TAIL_LRB + GLA_GATE_RANK].set(gla_w_gate_up[:, 1].astype(BF16))
    return w_main, w_tail, wq_p, wkv_p, (qgn, qgr, kgn, kgr), wg


def kernel(x, c, ctx, c_ctx, w_mod, b_mod, norm1_g, norm2_g, w_in, mla_q_norm_g, mla_w_q_up,
           mla_kv_norm_g, mla_w_kv_up, mla_q_head_g, mla_k_head_g, gla_w_gate_up, gla_b_gate,
           gla_out_norm_g, pool_w, pool_scale, w_branch, w_out, ffn_w1, ffn_w3, ffn_w2):
    batch, seq, d = x.shape
    n_ctx = ctx.shape[1]
    depth = w_mod.shape[0]
    assert batch + 1 <= 8 and seq % GLA_BLOCK == 0 and n_ctx % GLA_BLOCK == 0

    c8 = jnp.zeros((8, d), F32).at[:batch].set(c).at[batch].set(c_ctx)
    mods_all = _mods_call(c8, w_mod, b_mod).reshape(depth, 8 * N_MOD, 1, d)

    w_main, w_tail, wq_p, wkv_p, (qgn, qgr, kgn, kgr), wg = _prep_weights(
        w_in, mla_w_q_up, mla_w_kv_up, mla_q_head_g, mla_k_head_g, gla_w_gate_up)
    w_branch_b = w_branch.astype(BF16)
    w_out_b = w_out.astype(BF16)
    w1_b, w3_b, w2_b = ffn_w1.astype(BF16), ffn_w3.astype(BF16), ffn_w2.astype(BF16)
    pool_w_b = pool_w.astype(BF16)
    rope_tabs = _rope_tables(seq)
    gla_consts = _gla_constants()
    bands = _pool_bands()
    n_main = w_main.shape[1]

    tok_l = _Tokens(batch * seq, seq, 0)
    tok_c = _Tokens(batch * n_ctx, None, batch)
    xl = x.reshape(batch * seq, d)
    xc = ctx.reshape(batch * n_ctx, d)

    for l in range(depth):
        last = l == depth - 1
        mods = mods_all[l]
        g1 = norm1_g[l][None, :]
        g2 = norm2_g[l][None, :]
        gains = (mla_q_norm_g[l][None, :], mla_kv_norm_g[l][None, :], qgn[l], qgr[l], kgn[l], kgr[l])

        z_l, zt_l = _in_proj_call(tok_l, xl, g1, mods, w_main, w_tail, l, n_main, 1536, "in_proj")
        z_c, zt_c = _in_proj_call(tok_c, xc, g1, mods, w_main, w_tail, l, Z_KV_END if last else n_main,
                                  1536, "in_proj_ctx")

        q_l, k_l, v_l = _mla_prep_call(z_l, zt_l, wq_p, wkv_p, l, gains, rope_tabs, seq)
        q_c, k_c, v_c = _mla_prep_call(z_c, zt_c, wq_p, wkv_p, l, gains, None, n_ctx)
        mla_l = _attn_lat_call(q_l, k_l, v_l, k_c, v_c, batch, seq, n_ctx)

        gla_l, gla_c = _gla_call(z_l, zt_l, z_c, zt_c, wg[l], gla_b_gate[l][:, None, :],
                                 gla_out_norm_g[l][None, :], gla_consts, batch, seq, n_ctx, not last)
        pool_l, pool_c = _pool_call(z_l, None if last else z_c, bands, pool_w_b[l],
                                    pool_scale[l][None, :], batch, seq, n_ctx)

        def finish_layer(tok, xs, z, pool_o, mla_o, gla_o):
            merged = _merge_call(tok, pool_o, mla_o, gla_o, z, w_branch_b, l, d)
            xs = _matmul_residual_call(tok, merged, w_out_b, l, xs, mods, 2, 1024, "out_proj")
            up = _swiglu_up_call(tok, xs, g2, mods, w1_b, w3_b, l)
            return _matmul_residual_call(tok, up, w2_b, l, xs, mods, 5, 512, "swiglu_down")

        xl = finish_layer(tok_l, xl, z_l, pool_l, mla_l, gla_l)
        if not last:
            mla_c = _attn_ctx_call(q_c, k_c, v_c, batch, n_ctx)
            xc = finish_layer(tok_c, xc, z_c, pool_c, mla_c, gla_c)
    return xl.reshape(batch, seq, d)
```
